```python
import jax, jax.numpy as jnp
from jax import lax
import numpy as np

D_MODEL = 2048
BATCH = 4
SEQ = 4096
DEPTH = 2

N_EVEN = (DEPTH + 1) // 2
N_ODD = DEPTH // 2
PAR_WIDTH = D_MODEL
SB_WIDTH = PAR_WIDTH // 2
SB_HEADS = 8
SB_HEAD_DIM = SB_WIDTH // SB_HEADS
SB_BLOCK = 128
HG_WIDTH = PAR_WIDTH - SB_WIDTH
HG_HEADS = 8
HG_VDIM = HG_WIDTH // HG_HEADS
HG_KDIM = 128
HG_KEY_WIDTH = HG_HEADS * HG_KDIM
HG_CHUNK = 64
PAR_IN = 3 * SB_WIDTH + 2 * HG_KEY_WIDTH + 2 * HG_WIDTH
PAR_SPLITS = [SB_WIDTH, 2 * SB_WIDTH, 3 * SB_WIDTH, 3 * SB_WIDTH + HG_KEY_WIDTH,
              3 * SB_WIDTH + 2 * HG_KEY_WIDTH, 3 * SB_WIDTH + 2 * HG_KEY_WIDTH + HG_WIDTH]
SG_WIDTH = D_MODEL
SG_GROUPS = 8
SG_CHUNK = 128
FFN_DIM = 11 * D_MODEL // 4
CONV_WIDTH = 3
NORM_EPS = 1e-6

kernel_name = "hybrid_stickbreak_hgrn2_gmlp_convffn_adaln"


def rms_norm(x, gain):
    x32 = x.astype(jnp.float32)
    y = x32 * lax.rsqrt(jnp.mean(x32 * x32, axis=-1, keepdims=True) + NORM_EPS)
    return (y * gain.astype(jnp.float32)).astype(x.dtype)


def layer_norm(x, gain, bias):
    x32 = x.astype(jnp.float32)
    mu = jnp.mean(x32, axis=-1, keepdims=True)
    xc = x32 - mu
    y = xc * lax.rsqrt(jnp.mean(xc * xc, axis=-1, keepdims=True) + NORM_EPS)
    return (y * gain.astype(jnp.float32) + bias.astype(jnp.float32)).astype(x.dtype)


def stick_breaking_attention(q, k, v):
    S = q.shape[2]
    scale = q.shape[-1] ** -0.5
    outs = []
    for blk in range(S // SB_BLOCK):
        q0, q1 = blk * SB_BLOCK, (blk + 1) * SB_BLOCK
        z = jnp.einsum('bhtd,bhsd->bhts', q[:, :, q0:q1], k[:, :, :q1]).astype(jnp.float32) * scale
        t_idx = q0 + jnp.arange(SB_BLOCK)[:, None]
        s_idx = jnp.arange(q1)[None, :]
        strict = s_idx < t_idx
        log_keep = jnp.where(strict, jax.nn.log_sigmoid(-z), 0.0)
        tail = lax.cumsum(log_keep, axis=3, reverse=True) - log_keep
        w = jnp.where(strict, jnp.exp(jax.nn.log_sigmoid(z) + tail), 0.0)
        outs.append(jnp.einsum('bhts,bhsd->bhtd', w.astype(v.dtype), v[:, :, :q1]))
    return jnp.concatenate(outs, axis=2)


def hgrn2_recurrence(q, f_logit, i, g, lower_bound, out_gain):
    dtype = q.dtype
    Bn, S, H, dk = q.shape
    dv = i.shape[-1]
    n = S // HG_CHUNK
    lb = lower_bound.astype(jnp.float32)
    f = lb + (1.0 - lb) * jax.nn.sigmoid(f_logit.astype(jnp.float32))
    log_f = jnp.log(f)
    k = 1.0 - f
    qf = jax.nn.silu(q.astype(jnp.float32))

    def to_chunks(t):
        return t.transpose(0, 2, 1, 3).reshape(Bn, H, n, HG_CHUNK, t.shape[-1])

    qc, kc, vc, lc = (to_chunks(t) for t in (qf, k, i.astype(jnp.float32), log_f))
    G = jnp.cumsum(lc, axis=3)
    G_last = G[:, :, :, -1:, :]
    q_dec = qc * jnp.exp(G)
    k_inv = kc * jnp.exp(-G)
    k_end = kc * jnp.exp(G_last - G)
    causal = jnp.tril(jnp.ones((HG_CHUNK, HG_CHUNK), dtype=bool))
    scores = jnp.where(causal, jnp.einsum('bhntd,bhnsd->bhnts', q_dec, k_inv), 0.0)
    o_intra = jnp.einsum('bhnts,bhnsv->bhntv', scores, vc)

    def step(state, xs):
        q_d, k_e, v_c, decay = xs
        o = jnp.einsum('bhtd,bhdv->bhtv', q_d, state)
        state = decay[..., None] * state + jnp.einsum('bhtd,bhtv->bhdv', k_e, v_c)
        return state, o

    xs = tuple(jnp.moveaxis(t, 2, 0) for t in (q_dec, k_end, vc, jnp.exp(G_last[:, :, :, 0, :])))
    state0 = jnp.zeros((Bn, H, dk, dv), jnp.float32)
    _, o_inter = lax.scan(step, state0, xs)
    o = o_intra + jnp.moveaxis(o_inter, 0, 2)
    o = o.reshape(Bn, H, S, dv).transpose(0, 2, 1, 3)
    o = o * lax.rsqrt(jnp.mean(o * o, axis=-1, keepdims=True) + NORM_EPS) * out_gain.astype(jnp.float32)
    return (o * jax.nn.silu(g.astype(jnp.float32))).astype(dtype)


def parallel_mixer(h, w_in, w_out, lower_bound, hg_gain):
    Bn, S, _ = h.shape
    proj = h @ w_in
    sb_q, sb_k, sb_v, hg_q, hg_f, hg_i, hg_g = jnp.split(proj, PAR_SPLITS, axis=-1)
    bhsd = lambda t: t.reshape(Bn, S, SB_HEADS, SB_HEAD_DIM).transpose(0, 2, 1, 3)
    o_sb = stick_breaking_attention(bhsd(sb_q), bhsd(sb_k), bhsd(sb_v))
    o_sb = o_sb.transpose(0, 2, 1, 3).reshape(Bn, S, SB_WIDTH)
    bshd = lambda t: t.reshape(Bn, S, HG_HEADS, -1)
    o_hg = hgrn2_recurrence(bshd(hg_q), bshd(hg_f), bshd(hg_i), bshd(hg_g),
                            lower_bound.reshape(HG_HEADS, HG_KDIM), hg_gain)
    o_hg = o_hg.reshape(Bn, S, HG_WIDTH)
    return jnp.concatenate([o_sb, o_hg], axis=-1) @ w_out


def chunked_spatial_gating_mlp(h, w_in, v_gain, v_bias, w_pos, b_pos, w_out):
    Bn, S, _ = h.shape
    z = jax.nn.gelu(h @ w_in, approximate=False)
    u, v = jnp.split(z, 2, axis=-1)
    v = layer_norm(v, v_gain, v_bias)
    n = S // SG_CHUNK
    vg = v.reshape(Bn, n, SG_CHUNK, SG_GROUPS, SG_WIDTH // SG_GROUPS)
    causal = jnp.tril(jnp.ones((SG_CHUNK, SG_CHUNK), dtype=bool))
    w = jnp.where(causal, w_pos, 0.0).astype(v.dtype)
    mixed = jnp.einsum('gts,bnsgc->bntgc', w, vg) + b_pos.T[None, None, :, :, None]
    return (u * mixed.reshape(Bn, S, SG_WIDTH)) @ w_out


def conv_ffn(h, w_up, conv_w, conv_b, w_down):
    S = h.shape[1]
    a = h @ w_up
    ap = jnp.pad(a, ((0, 0), (CONV_WIDTH - 1, 0), (0, 0)))
    a = conv_b + sum(conv_w[j] * ap[:, j:j + S] for j in range(CONV_WIDTH))
    gate, val = jnp.split(a, 2, axis=-1)
    return (jax.nn.silu(gate) * val) @ w_down


def setup_inputs(seed: int = 0) -> dict:
    key = jax.random.key(seed)
    ks = jax.random.split(key, 24)
    f32 = jnp.float32
    nrm = lambda k, shape, scale: jax.random.normal(k, shape, f32) * scale
    gain = lambda k, shape: 1.0 + 0.05 * jax.random.normal(k, shape, f32)
    D = D_MODEL
    return {
        "x": nrm(ks[0], (BATCH, SEQ, D), 1.0),
        "c": nrm(ks[1], (BATCH, D), 1.0),
        "ada_w": nrm(ks[2], (DEPTH, D, 6 * D), D ** -0.5),
        "ada_b": nrm(ks[3], (DEPTH, 6 * D), 0.01),
        "mix_norm": gain(ks[4], (DEPTH, D)),
        "ffn_norm": gain(ks[5], (DEPTH, D)),
        "par_w_in": nrm(ks[6], (N_EVEN, D, PAR_IN), D ** -0.5),
        "par_w_out": nrm(ks[7], (N_EVEN, PAR_WIDTH, D), PAR_WIDTH ** -0.5),
        "hg_lb_logits": nrm(ks[8], (N_EVEN + 1, HG_KEY_WIDTH), 0.1),
        "hg_out_norm": gain(ks[9], (N_EVEN, HG_HEADS, HG_VDIM)),
        "sg_w_in": nrm(ks[10], (N_ODD, D, 2 * SG_WIDTH), D ** -0.5),
        "sg_v_gain": gain(ks[11], (N_ODD, SG_WIDTH)),
        "sg_v_bias": nrm(ks[12], (N_ODD, SG_WIDTH), 0.02),
        "sg_w_pos": nrm(ks[13], (N_ODD, SG_GROUPS, SG_CHUNK, SG_CHUNK), SG_CHUNK ** -0.5),
        "sg_b_pos": nrm(ks[14], (N_ODD, SG_GROUPS, SG_CHUNK), 0.02),
        "sg_w_out": nrm(ks[15], (N_ODD, SG_WIDTH, D), SG_WIDTH ** -0.5),
        "ffn_up": nrm(ks[16], (DEPTH, D, 2 * FFN_DIM), D ** -0.5),
        "ffn_conv_w": nrm(ks[17], (DEPTH, CONV_WIDTH, 2 * FFN_DIM), CONV_WIDTH ** -0.5),
        "ffn_conv_b": nrm(ks[18], (DEPTH, 2 * FFN_DIM), 0.02),
        "ffn_down": nrm(ks[19], (DEPTH, FFN_DIM, D), FFN_DIM ** -0.5),
        "final_norm": gain(ks[20], (D,)),
    }


def reference(x, c, ada_w, ada_b, mix_norm, ffn_norm, par_w_in, par_w_out, hg_lb_logits,
              hg_out_norm, sg_w_in, sg_v_gain, sg_v_bias, sg_w_pos, sg_b_pos, sg_w_out,
              ffn_up, ffn_conv_w, ffn_conv_b, ffn_down, final_norm):
    cond = jax.nn.silu(c)
    lower_bounds = jnp.cumsum(jax.nn.softmax(hg_lb_logits.astype(jnp.float32), axis=0), axis=0)
    for layer in range(DEPTH):
        j = layer // 2
        mod = cond @ ada_w[layer] + ada_b[layer]
        sh1, sc1, g1, sh2, sc2, g2 = jnp.split(mod[:, None, :], 6, axis=-1)
        h = rms_norm(x, mix_norm[layer]) * (1.0 + sc1) + sh1
        if layer % 2 == 0:
            y = parallel_mixer(h, par_w_in[j], par_w_out[j], lower_bounds[j], hg_out_norm[j])
        else:
            y = chunked_spatial_gating_mlp(h, sg_w_in[j], sg_v_gain[j], sg_v_bias[j],
                                           sg_w_pos[j], sg_b_pos[j], sg_w_out[j])
        x = x + g1 * y
        h = rms_norm(x, ffn_norm[layer]) * (1.0 + sc2) + sh2
        x = x + g2 * conv_ffn(h, ffn_up[layer], ffn_conv_w[layer], ffn_conv_b[layer], ffn_down[layer])
    return rms_norm(x, final_norm)
```

```python
import functools

import jax
import jax.numpy as jnp
from jax import lax
from jax.experimental import pallas as pl
from jax.experimental.pallas import tpu as pltpu

F32 = jnp.float32
BF16 = jnp.bfloat16

NORM_EPS = 1e-6
SB_HEADS = 8
HG_HEADS = 8
HG_CHUNK = 64
SG_GROUPS = 8
SG_CHUNK = 128
CONV_WIDTH = 3
HALO_ROWS = 8

ROW_TILE = 1024
SG_ROW_TILE = 512
SB_QUERY_TILE = 256
HG_TIME_TILE = 512

V7X_VMEM_LIMIT_BYTES =56 * 1024 * 1024
ADA_PAD_ROWS = 16


def _params(*sem):
    return pltpu.CompilerParams(dimension_semantics=sem, vmem_limit_bytes=V7X_VMEM_LIMIT_BYTES)


def _silu(x):
    return x * jax.nn.sigmoid(x)


def _dot(a, b):
    return jnp.dot(a, b, preferred_element_type=F32)


def _dot_nt(a, b):
    return lax.dot_general(a, b, (((1,), (1,)), ((), ())), preferred_element_type=F32)


def _split_dot(a_f32, b_bf16):
    hi = a_f32.astype(BF16)
    lo = (a_f32 - hi.astype(F32)).astype(BF16)
    return _dot(hi, b_bf16) + _dot(lo, b_bf16)


def _split_dot_left(b_bf16, a_f32):
    hi = a_f32.astype(BF16)
    lo = (a_f32 - hi.astype(F32)).astype(BF16)
    return _dot(b_bf16, hi) + _dot(b_bf16, lo)


def _modulated_rms_norm(x, gain, scale, shift):
    y = x * lax.rsqrt(jnp.mean(x * x, axis=-1, keepdims=True) + NORM_EPS) * gain
    return y * (1.0 + scale) + shift


def _ada_kernel(c_ref, w_ref, b_ref, o_ref):
    cond = _silu(c_ref[...]).astype(BF16)
    o_ref[...] = _dot(cond, w_ref[...].astype(BF16)) + b_ref[...]


def _ada_mod(c_pad, ada_w, ada_b):
    depth, d, n = ada_w.shape
    tn = 1024
    return pl.pallas_call(
        _ada_kernel,
        grid=(depth, n // tn),
        in_specs=[
            pl.BlockSpec((ADA_PAD_ROWS, d), lambda l, j: (0, 0)),
            pl.BlockSpec((None, d, tn), lambda l, j: (l, 0, j)),
            pl.BlockSpec((None, 1, tn), lambda l, j: (l, 0, j)),
        ],
        out_specs=pl.BlockSpec((None, ADA_PAD_ROWS, tn), lambda l, j: (l, 0, j)),
        out_shape=jax.ShapeDtypeStruct((depth, ADA_PAD_ROWS, n), F32),
        compiler_params=_params("parallel", "arbitrary"),
        name="ada_mod",
    )(c_pad, ada_w, ada_b.reshape(depth, 1, n))


def _mod_spec(layer, slot, d, bn, tiles_per_seq, with_j):
    nb = d // bn
    if with_j:
        return pl.BlockSpec((None, None, 1, bn),
                            lambda i, j: (layer, i // tiles_per_seq, 0, slot * nb + j))
    return pl.BlockSpec((None, None, 1, bn),
                        lambda i, j: (layer, i // tiles_per_seq, 0, slot * nb))


def _norm_matmul_kernel(x_ref, gain_ref, sc_ref, sh_ref, w_ref, o_ref, h_ref):
    @pl.when(pl.program_id(1) == 0)
    def _():
        h_ref[...] = _modulated_rms_norm(x_ref[...], gain_ref[...], sc_ref[...],
                                         sh_ref[...]).astype(BF16)

    o_ref[...] = _dot(h_ref[...], w_ref[...]).astype(o_ref.dtype)


def _norm_matmul(x2, gain, mod4, layer, slots, w, out_dtype, seq, bm, bn, name):
    t, d = x2.shape
    n = w.shape[1]
    tps = seq // bm
    return pl.pallas_call(
        _norm_matmul_kernel,
        grid=(t // bm, n // bn),
        in_specs=[
            pl.BlockSpec((bm, d), lambda i, j: (i, 0)),
            pl.BlockSpec((1, d), lambda i, j: (0, 0)),
            _mod_spec(layer, slots[0], d, d, tps, False),
            _mod_spec(layer, slots[1], d, d, tps, False),
            pl.BlockSpec((d, bn), lambda i, j: (0, j)),
        ],
        out_specs=pl.BlockSpec((bm, bn), lambda i, j: (i, j)),
        out_shape=jax.ShapeDtypeStruct((t, n), out_dtype),
        scratch_shapes=[pltpu.VMEM((bm, d), BF16)],
        compiler_params=_params("parallel", "arbitrary"),
        name=name,
    )(x2, gain, mod4, mod4, w)


def _sb_kernel(q_ref, k_ref, v_ref, o_ref, *, tq, scale):
    qi = pl.program_id(2)
    q = q_ref[...]
    row = lax.broadcasted_iota(jnp.int32, (tq, tq), 0)
    col = lax.broadcasted_iota(jnp.int32, (tq, tq), 1)
    strict = col < row
    after = (row > col).astype(BF16)

    def block(kb, carry, acc, masked):
        start = pl.multiple_of(kb * tq, tq)
        k = k_ref[pl.ds(start, tq), :]
        v = v_ref[pl.ds(start, tq), :]
        z = _dot_nt(q, k) * scale
        log_keep = -(jnp.maximum(z, 0.0) + jnp.log(1.0 + jnp.exp(-jnp.abs(z))))
        if masked:
            log_keep = jnp.where(strict, log_keep, 0.0)
        tail = _split_dot(log_keep, after)
        w = jnp.exp(z + log_keep + tail + carry)
        if masked:
            w = jnp.where(strict, w, 0.0)
        acc = acc + _dot(w.astype(BF16), v)
        carry = carry + tail[:, :1] + log_keep[:, :1]
        return carry, acc

    carry0 = jnp.zeros((tq, 1), F32)
    acc0 = jnp.zeros((tq, q.shape[1]), F32)
    carry, acc = block(qi, carry0, acc0, True)

    def body(step, state):
        return block(qi - 1 - step, state[0], state[1], False)

    _, acc = lax.fori_loop(0, qi, body, (carry, acc))
    o_ref[...] = acc.astype(o_ref.dtype)


def _sb_attention(qkv, heads, tq):
    b, s, w3 = qkv.shape
    dh = w3 // (3 * heads)
    kern = functools.partial(_sb_kernel, tq=tq, scale=dh ** -0.5)
    return pl.pallas_call(
        kern,
        grid=(b, heads, s // tq),
        in_specs=[
            pl.BlockSpec((None, tq, dh), lambda bi, h, qi: (bi, qi, h)),
            pl.BlockSpec((None, s, dh), lambda bi, h, qi: (bi, 0, heads + h)),
            pl.BlockSpec((None, s, dh), lambda bi, h, qi: (bi, 0, 2 * heads + h)),
        ],
        out_specs=pl.BlockSpec((None, tq, dh), lambda bi, h, qi: (bi, qi, h)),
        out_shape=jax.ShapeDtypeStruct((b, s, heads * dh), BF16),
        compiler_params=_params("parallel", "parallel", "arbitrary"),
        name="sb_attention",
    )(qkv, qkv, qkv)


def _hgrn_kernel(lbl_ref, gain_ref, q_ref, f_ref, i_ref, g_ref, o_ref, st_ref, *,
                 layer_j, heads, chunk, tt):
    @pl.when(pl.program_id(1) == 0)
    def _():
        st_ref[...] = jnp.zeros_like(st_ref)

    logits = lbl_ref[...]
    e = jnp.exp(logits - jnp.max(logits, axis=0, keepdims=True))
    sm = e / jnp.sum(e, axis=0, keepdims=True)
    lb = jnp.sum(sm[:layer_j + 1], axis=0, keepdims=True)
    gain = gain_ref[...]

    dk = q_ref.shape[1] // heads
    dv = i_ref.shape[1] // heads
    trow = lax.broadcasted_iota(jnp.int32, (chunk, chunk), 0)
    tcol = lax.broadcasted_iota(jnp.int32, (chunk, chunk), 1)
    causal = tcol <= trow
    lower = causal.astype(BF16)

    def chunk_body(c, _):
        r0 = pl.multiple_of(c * chunk, chunk)
        rows = pl.ds(r0, chunk)
        f = lb + (1.0 - lb) * jax.nn.sigmoid(f_ref[rows, :])
        log_f = jnp.log(f)
        k = 1.0 - f
        qf = _silu(q_ref[rows, :])
        g_cum = _split_dot_left(lower, log_f)
        g_last = g_cum[chunk - 1:chunk, :]
        q_dec = (qf * jnp.exp(g_cum)).astype(BF16)
        k_inv = (k * jnp.exp(-g_cum)).astype(BF16)
        k_end = (k * jnp.exp(g_last - g_cum)).astype(BF16)
        decay = jnp.exp(g_last)
        v_all = i_ref[rows, :]
        gate = _silu(g_ref[rows, :])
        for h in range(heads):
            ks = slice(h * dk, (h + 1) * dk)
            vs = slice(h * dv, (h + 1) * dv)
            v = v_all[:, vs].astype(BF16)
            v_t = v_all[:, vs].T.astype(BF16)
            scores = jnp.where(causal, _dot_nt(q_dec[:, ks], k_inv[:, ks]), 0.0)
            st = st_ref[h]
            o = _dot(scores.astype(BF16), v) + _dot_nt(q_dec[:, ks], st.astype(BF16))
            st_ref[h] = decay[:, ks] * st + _dot(v_t, k_end[:, ks])
            o = o * lax.rsqrt(jnp.mean(o * o, axis=-1, keepdims=True) + NORM_EPS) * gain[:, vs]
            o_ref[rows, vs] = (o * gate[:, vs]).astype(o_ref.dtype)
        return 0

    lax.fori_loop(0, tt // chunk, chunk_body, 0)


def _hgrn2(hg, lb_logits, out_gain, layer_j, heads, tt):
    b, s, w4 = hg.shape
    kw = lb_logits.shape[1]
    vw = out_gain.shape[1]
    assert w4 == 2 * kw + 2 * vw and kw == vw
    nb = w4 // kw
    dk = kw // heads
    dv = vw // heads
    kern = functools.partial(_hgrn_kernel, layer_j=layer_j, heads=heads, chunk=HG_CHUNK, tt=tt)
    col = lambda k: pl.BlockSpec((None, tt, kw), lambda bi, ti: (bi, ti, k))
    assert nb == 4
    return pl.pallas_call(
        kern,
        grid=(b, s // tt),
        in_specs=[
            pl.BlockSpec(lb_logits.shape, lambda bi, ti: (0, 0)),
            pl.BlockSpec((1, vw), lambda bi, ti: (0, 0)),
            col(0), col(1), col(2), col(3),
        ],
        out_specs=pl.BlockSpec((None, tt, vw), lambda bi, ti: (bi, ti, 0)),
        out_shape=jax.ShapeDtypeStruct((b, s, vw), BF16),
        scratch_shapes=[pltpu.VMEM((heads, dv, dk), F32)],
        compiler_params=_params("parallel", "arbitrary"),
        name="hgrn2",
    )(lb_logits, out_gain, hg, hg, hg, hg)


def _residual_matmul_kernel(*refs, n_lhs):
    a_refs = refs[:n_lhs]
    w_refs = refs[n_lhs:2 * n_lhs]
    x_ref, g_ref, o_ref = refs[2 * n_lhs:]
    acc = _dot(a_refs[0][...], w_refs[0][...])
    for a_ref, w_ref in zip(a_refs[1:], w_refs[1:]):
        acc = acc + _dot(a_ref[...], w_ref[...])
    o_ref[...] = x_ref[...] + g_ref[...] * acc


def _residual_matmul(lhs_list, w, x2, mod4, layer, slot, seq, bm, bn, name):
    t, d = x2.shape
    n_lhs = len(lhs_list)
    tps = seq // bm
    in_specs = []
    row0 = 0
    w_specs = []
    for a in lhs_list:
        ka = a.shape[1]
        in_specs.append(pl.BlockSpec((bm, ka), lambda i, j: (i, 0)))
        assert row0 % ka == 0
        w_specs.append(pl.BlockSpec((ka, bn), functools.partial(lambda i, j, r: (r, j), r=row0 // ka)))
        row0 += ka
    assert row0 == w.shape[0]
    in_specs += w_specs
    in_specs += [
        pl.BlockSpec((bm, bn), lambda i, j: (i, j)),
        _mod_spec(layer, slot, d, bn, tps, True),
    ]
    return pl.pallas_call(
        functools.partial(_residual_matmul_kernel, n_lhs=n_lhs),
        grid=(t // bm, d // bn),
        in_specs=in_specs,
        out_specs=pl.BlockSpec((bm, bn), lambda i, j: (i, j)),
        out_shape=jax.ShapeDtypeStruct((t, d), F32),
        compiler_params=_params("parallel", "arbitrary"),
        name=name,
    )(*lhs_list, *([w] * n_lhs), x2, mod4)


def _ffn_up_kernel(x_ref, xh_ref, gain_ref, sc_ref, sh_ref, wg_ref, wv_ref, cwg_ref, cwv_ref,
                   cbg_ref, cbv_ref, o_ref, h_ref, hh_ref, eg_ref, ev_ref, *, bm, tiles_per_seq):
    i = pl.program_id(0)

    @pl.when(pl.program_id(1) == 0)
    def _():
        gain, sc, sh = gain_ref[...], sc_ref[...], sh_ref[...]
        h_ref[...] = _modulated_rms_norm(x_ref[...], gain, sc, sh).astype(BF16)
        hh_ref[...] = _modulated_rms_norm(xh_ref[...], gain, sc, sh).astype(BF16)

    first = (i % tiles_per_seq) == 0
    h = h_ref[...]
    hh = hh_ref[...]
    for w_ref, e_ref in ((wg_ref, eg_ref), (wv_ref, ev_ref)):
        w = w_ref[...]
        e_ref[pl.ds(0, HALO_ROWS), :] = jnp.where(first, 0.0, _dot(hh, w))
        e_ref[pl.ds(HALO_ROWS, bm), :] = _dot(h, w)

    def conv(e_ref, cw_ref, cb_ref):
        out = cb_ref[...]
        for tap in range(CONV_WIDTH):
            off = HALO_ROWS - (CONV_WIDTH - 1) + tap
            out = out + cw_ref[tap:tap + 1, :] * e_ref[pl.ds(off, bm), :]
        return out

    gate = conv(eg_ref, cwg_ref, cbg_ref)
    val = conv(ev_ref, cwv_ref, cbv_ref)
    o_ref[...] = (_silu(gate) * val).astype(o_ref.dtype)


def _ffn_up(x2, gain, mod4, layer, w_up, conv_w, conv_b, seq, bm, bn):
    t, d = x2.shape
    f = w_up.shape[1] // 2
    nf = f // bn
    tps = seq // bm
    hb = bm // HALO_ROWS
    depth = conv_w.shape[0]
    conv_b3 = conv_b.reshape(depth, 1, 2 * f)
    kern = functools.partial(_ffn_up_kernel, bm=bm, tiles_per_seq=tps)
    return pl.pallas_call(
        kern,
        grid=(t // bm, nf),
        in_specs=[
            pl.BlockSpec((bm, d), lambda i, j: (i, 0)),
            pl.BlockSpec((HALO_ROWS, d), lambda i, j: (jnp.maximum(i * hb - 1, 0), 0)),
            pl.BlockSpec((1, d), lambda i, j: (0, 0)),
            _mod_spec(layer, 4, d, d, tps, False),
            _mod_spec(layer, 3, d, d, tps, False),
            pl.BlockSpec((d, bn), lambda i, j: (0, j)),
            pl.BlockSpec((d, bn), lambda i, j: (0, nf + j)),
            pl.BlockSpec((None, CONV_WIDTH, bn), lambda i, j: (layer, 0, j)),
            pl.BlockSpec((None, CONV_WIDTH, bn), lambda i, j: (layer, 0, nf + j)),
            pl.BlockSpec((None, 1, bn), lambda i, j: (layer, 0, j)),
            pl.BlockSpec((None, 1, bn), lambda i, j: (layer, 0, nf + j)),
        ],
        out_specs=pl.BlockSpec((bm, bn), lambda i, j: (i, j)),
        out_shape=jax.ShapeDtypeStruct((t, f), BF16),
        scratch_shapes=[
            pltpu.VMEM((bm, d), BF16),
            pltpu.VMEM((HALO_ROWS, d), BF16),
            pltpu.VMEM((bm + HALO_ROWS, bn), F32),
            pltpu.VMEM((bm + HALO_ROWS, bn), F32),
        ],
        compiler_params=_params("parallel", "arbitrary"),
        name="ffn_up",
    )(x2, x2, gain, mod4, mod4, w_up, w_up, conv_w, conv_w, conv_b3, conv_b3)


def _gelu(x):
    return 0.5 * x * (1.0 + lax.erf(x * (2.0 ** -0.5)))


def _sg_in_kernel(x_ref, gain_ref, sc_ref, sh_ref, w_ref, vg_ref, vb_ref, u_ref, v_ref, h_ref):
    j = pl.program_id(1)

    @pl.when(j == 0)
    def _():
        h_ref[...] = _modulated_rms_norm(x_ref[...], gain_ref[...], sc_ref[...],
                                         sh_ref[...]).astype(BF16)

    z = _gelu(_dot(h_ref[...], w_ref[...]))

    @pl.when(j == 0)
    def _():
        u_ref[...] = z.astype(u_ref.dtype)

    @pl.when(j == 1)
    def _():
        zc = z - jnp.mean(z, axis=-1, keepdims=True)
        y = zc * lax.rsqrt(jnp.mean(zc * zc, axis=-1, keepdims=True) + NORM_EPS)
        v_ref[...] = (y * vg_ref[...] + vb_ref[...]).astype(v_ref.dtype)


def _sg_in(x2, gain, mod4, layer, w_in, v_gain, v_bias, seq, bm):
    t, d = x2.shape
    width = w_in.shape[1] // 2
    tps = seq // bm
    out = jax.ShapeDtypeStruct((t, width), BF16)
    return pl.pallas_call(
        _sg_in_kernel,
        grid=(t // bm, 2),
        in_specs=[
            pl.BlockSpec((bm, d), lambda i, j: (i, 0)),
            pl.BlockSpec((1, d), lambda i, j: (0, 0)),
            _mod_spec(layer, 1, d, d, tps, False),
            _mod_spec(layer, 0, d, d, tps, False),
            pl.BlockSpec((d, width), lambda i, j: (0, j)),
            pl.BlockSpec((1, width), lambda i, j: (0, 0)),
            pl.BlockSpec((1, width), lambda i, j: (0, 0)),
        ],
        out_specs=[pl.BlockSpec((bm, width), lambda i, j: (i, 0)),
                   pl.BlockSpec((bm, width), lambda i, j: (i, 0))],
        out_shape=[out, out],
        scratch_shapes=[pltpu.VMEM((bm, d), BF16)],
        compiler_params=_params("parallel", "arbitrary"),
        name="sg_in",
    )(x2, gain, mod4, mod4, w_in, v_gain, v_bias)


def _sg_out_kernel(u_ref, v_ref, wp_ref, bp_ref, w_ref, x_ref, g_ref, o_ref, m_ref, *,
                   bm, groups, chunk):
    @pl.when(pl.program_id(1) == 0)
    def _():
        gw = u_ref.shape[1] // groups
        row = lax.broadcasted_iota(jnp.int32, (chunk, chunk), 0)
        col = lax.broadcasted_iota(jnp.int32, (chunk, chunk), 1)
        causal = col <= row
        for g in range(groups):
            w_pos = jnp.where(causal, wp_ref[g], 0.0).astype(BF16)
            bias = bp_ref[:, g:g + 1]
            cs = slice(g * gw, (g + 1) * gw)
            for r in range(bm // chunk):
                rs = slice(r * chunk, (r + 1) * chunk)
                mixed = _dot(w_pos, v_ref[rs, cs]) + bias
                m_ref[rs, cs] = (u_ref[rs, cs].astype(F32) * mixed).astype(BF16)

    o_ref[...] = x_ref[...] + g_ref[...] * _dot(m_ref[...], w_ref[...])


def _sg_out(u, v, w_pos, b_pos_t, w_out, x2, mod4, layer, seq, bm, bn):
    t, d = x2.shape
    width = u.shape[1]
    tps = seq // bm
    kern = functools.partial(_sg_out_kernel, bm=bm, groups=SG_GROUPS, chunk=SG_CHUNK)
    return pl.pallas_call(
        kern,
        grid=(t // bm, d // bn),
        in_specs=[
            pl.BlockSpec((bm, width), lambda i, j: (i, 0)),
            pl.BlockSpec((bm, width), lambda i, j: (i, 0)),
            pl.BlockSpec(w_pos.shape, lambda i, j: (0, 0, 0)),
            pl.BlockSpec(b_pos_t.shape, lambda i, j: (0, 0)),
            pl.BlockSpec((width, bn), lambda i, j: (0, j)),
            pl.BlockSpec((bm, bn), lambda i, j: (i, j)),
            _mod_spec(layer, 2, d, bn, tps, True),
        ],
        out_specs=pl.BlockSpec((bm, bn), lambda i, j: (i, j)),
        out_shape=jax.ShapeDtypeStruct((t, d), F32),
        scratch_shapes=[pltpu.VMEM((bm, width), BF16)],
        compiler_params=_params("parallel", "arbitrary"),
        name="sg_out",
    )(u, v, w_pos, b_pos_t, w_out, x2, mod4)


def _final_norm_kernel(x_ref, gain_ref, o_ref):
    x = x_ref[...]
    o_ref[...] = x * lax.rsqrt(jnp.mean(x * x, axis=-1, keepdims=True) + NORM_EPS) * gain_ref[...]


def _final_norm(x2, gain, bm):
    t, d = x2.shape
    return pl.pallas_call(
        _final_norm_kernel,
        grid=(t // bm,),
        in_specs=[pl.BlockSpec((bm, d), lambda i: (i, 0)),
                  pl.BlockSpec((1, d), lambda i: (0, 0))],
        out_specs=pl.BlockSpec((bm, d), lambda i: (i, 0)),
        out_shape=jax.ShapeDtypeStruct((t, d), F32),
        compiler_params=_params("parallel"),
        name="final_norm",
    )(x2, gain)


def kernel(x, c, ada_w, ada_b, mix_norm, ffn_norm, par_w_in, par_w_out, hg_lb_logits, hg_out_norm, sg_w_in, sg_v_gain, sg_v_bias, sg_w_pos, sg_b_pos, sg_w_out, ffn_up, ffn_conv_w, ffn_conv_b, ffn_down, final_norm):
    b, s, d = x.shape
    depth = ada_w.shape[0]
    t = b * s
    sb_w = d // 2
    bm = min(ROW_TILE, s)
    bm_sg = min(SG_ROW_TILE, s)

    c_pad = jnp.pad(c, ((0, ADA_PAD_ROWS - b), (0, 0)))
    mod = _ada_mod(c_pad, ada_w, ada_b)
    mod4 = mod.reshape(depth, ADA_PAD_ROWS, 1, 6 * d)

    x2 = x.reshape(t, d)
    for layer in range(depth):
        j = layer // 2
        gain1 = mix_norm[layer].reshape(1, d)
        if layer % 2 == 0:
            w_in = par_w_in[j].astype(BF16)
            qkv = _norm_matmul(x2, gain1, mod4, layer, (1, 0), w_in[:, :3 * sb_w], BF16,
                               s, bm, 1024, "sb_proj")
            hg = _norm_matmul(x2, gain1, mod4, layer, (1, 0), w_in[:, 3 * sb_w:], F32,
                              s, bm, 1024, "hg_proj")
            o_sb = _sb_attention(qkv.reshape(b, s, 3 * sb_w), SB_HEADS, min(SB_QUERY_TILE, s))
            o_hg = _hgrn2(hg.reshape(b, s, hg.shape[1]), hg_lb_logits,
                          hg_out_norm[j].reshape(1, -1), j, HG_HEADS, min(HG_TIME_TILE, s))
            x2 = _residual_matmul([o_sb.reshape(t, sb_w), o_hg.reshape(t, -1)],
                                  par_w_out[j].astype(BF16), x2, mod4, layer, 2, s, bm, 1024,
                                  "par_out")
        else:
            u, v = _sg_in(x2, gain1, mod4, layer, sg_w_in[j].astype(BF16),
                          sg_v_gain[j].reshape(1, -1), sg_v_bias[j].reshape(1, -1), s, bm_sg)
            x2 = _sg_out(u, v, sg_w_pos[j], sg_b_pos[j].T, sg_w_out[j].astype(BF16), x2, mod4,
                         layer, s, bm_sg, 1024)
        act = _ffn_up(x2, ffn_norm[layer].reshape(1, d), mod4, layer, ffn_up[layer].astype(BF16),
                      ffn_conv_w, ffn_conv_b, s, bm, 512)
        x2 = _residual_matmul([act], ffn_down[layer].astype(BF16), x2, mod4, layer, 5, s, bm, 512,
                              "ffn_down")
    out = _final_norm(x2, final_norm.reshape(1, d), bm_sg)
    return out.reshape(b, s, d)
```

```python
import functools

import jax
import jax.numpy as jnp
from jax import lax
from jax.experimental import pallas as pl
from jax.experimental.pallas import tpu as pltpu

F32 = jnp.float32
BF16 = jnp.bfloat16

NORM_EPS = 1e-6
SB_HEADS = 8
HG_HEADS = 8
HG_CHUNK = 64
SG_GROUPS = 8
SG_CHUNK = 128
CONV_WIDTH = 3
HALO_ROWS = 8

LOG2E = 1.4426950408889634
NORM_ROW_BLOCK = 16
SB_HEADS_PER_STEP = 8
EPILOGUE_ROWS = 16
EPILOGUE_LANES = 128
FFN_DOT_ROWS = 256
MXU_COLUMNS = 256
ROW_TILE = 1024
SG_ROW_TILE = 512
SB_QUERY_TILE = 256
HG_TIME_TILE = 512

V7X_VMEM_LIMIT_BYTES =56 * 1024 * 1024
ADA_PAD_ROWS = 16


def _params(*sem):
    return pltpu.CompilerParams(dimension_semantics=sem, vmem_limit_bytes=V7X_VMEM_LIMIT_BYTES)


def _silu(x):
    return x * jax.nn.sigmoid(x)


def _dot(a, b):
    return jnp.dot(a, b, preferred_element_type=F32)


def _dot_nt(a, b):
    return lax.dot_general(a, b, (((1,), (1,)), ((), ())), preferred_element_type=F32)


def _split_dot(a_f32, b_bf16):
    hi = a_f32.astype(BF16)
    lo = (a_f32 - hi.astype(F32)).astype(BF16)
    return _dot(hi, b_bf16) + _dot(lo, b_bf16)


def _split_dot_left(b_bf16, a_f32):
    hi = a_f32.astype(BF16)
    lo = (a_f32 - hi.astype(F32)).astype(BF16)
    return _dot(b_bf16, hi) + _dot(b_bf16, lo)


def _modulated_rms_norm(x, gain_scale, shift):
    inv = lax.rsqrt(jnp.mean(x * x, axis=-1, keepdims=True) + NORM_EPS)
    return x * inv * gain_scale + shift


def _norm_rows(x_ref, h_ref, gain_ref, sc_ref, sh_ref):
    rows = x_ref.shape[0]
    rb = min(NORM_ROW_BLOCK, rows)
    gain_scale = gain_ref[...] * (1.0 + sc_ref[...])
    shift = sh_ref[...]

    def body(r, _):
        rs = pl.ds(pl.multiple_of(r * rb, rb), rb)
        h_ref[rs, :] = _modulated_rms_norm(x_ref[rs, :], gain_scale, shift).astype(h_ref.dtype)
        return 0

    lax.fori_loop(0, rows // rb, body, 0)


def _ada_kernel(c_ref, w_ref, b_ref, o_ref):
    cond = _silu(c_ref[...]).astype(BF16)
    o_ref[...] = _dot(cond, w_ref[...].astype(BF16)) + b_ref[...]


def _ada_mod(c_pad, ada_w, ada_b):
    depth, d, n = ada_w.shape
    tn = 1024
    return pl.pallas_call(
        _ada_kernel,
        grid=(depth, n // tn),
        in_specs=[
            pl.BlockSpec((ADA_PAD_ROWS, d), lambda l, j: (0, 0)),
            pl.BlockSpec((None, d, tn), lambda l, j: (l, 0, j)),
            pl.BlockSpec((None, 1, tn), lambda l, j: (l, 0, j)),
        ],
        out_specs=pl.BlockSpec((None, ADA_PAD_ROWS, tn), lambda l, j: (l, 0, j)),
        out_shape=jax.ShapeDtypeStruct((depth, ADA_PAD_ROWS, n), F32),
        compiler_params=_params("parallel", "arbitrary"),
        name="ada_mod",
    )(c_pad, ada_w, ada_b.reshape(depth, 1, n))


def _mod_spec(layer, slot, d, bn, tiles_per_seq, with_j):
    nb = d // bn
    if with_j:
        return pl.BlockSpec((None, None, 1, bn),
                            lambda i, j: (layer, i // tiles_per_seq, 0, slot * nb + j))
    return pl.BlockSpec((None, None, 1, bn),
                        lambda i, j: (layer, i // tiles_per_seq, 0, slot * nb))


def _par_proj_kernel(x_ref, gain_ref, sc_ref, sh_ref, w_ref, qkv_ref, hg_ref, h_ref, *, n_sb):
    j = pl.program_id(1)

    @pl.when(j == 0)
    def _():
        _norm_rows(x_ref, h_ref, gain_ref, sc_ref, sh_ref)

    @pl.when(j < n_sb)
    def _():
        qkv_ref[...] = _dot(h_ref[...], w_ref[...]).astype(qkv_ref.dtype)

    @pl.when(j >= n_sb)
    def _():
        hg_ref[...] = _dot(h_ref[...], w_ref[...])


def _par_proj(x2, gain, mod4, layer, w, sb_cols, seq, bm, bn):
    t, d = x2.shape
    n = w.shape[1]
    tps = seq // bm
    n_sb = sb_cols // bn
    return pl.pallas_call(
        functools.partial(_par_proj_kernel, n_sb=n_sb),
        grid=(t // bm, n // bn),
        in_specs=[
            pl.BlockSpec((bm, d), lambda i, j: (i, 0)),
            pl.BlockSpec((1, d), lambda i, j: (0, 0)),
            _mod_spec(layer, 1, d, d, tps, False),
            _mod_spec(layer, 0, d, d, tps, False),
            pl.BlockSpec((d, bn), lambda i, j: (0, j)),
        ],
        out_specs=[
            pl.BlockSpec((bm, bn), lambda i, j: (i, jnp.minimum(j, n_sb - 1))),
            pl.BlockSpec((bm, bn), lambda i, j: (i, jnp.maximum(j - n_sb, 0))),
        ],
        out_shape=[jax.ShapeDtypeStruct((t, sb_cols), BF16),
                   jax.ShapeDtypeStruct((t, n - sb_cols), F32)],
        scratch_shapes=[pltpu.VMEM((bm, d), BF16)],
        compiler_params=_params("parallel", "arbitrary"),
        name="par_proj",
    )(x2, gain, mod4, mod4, w)


def _sb_kernel(q_ref, k_ref, v_ref, o_ref, acc_ref, *, tq, hp, dh, neg_scale_log2e):
    qi = pl.program_id(2)
    row = lax.broadcasted_iota(jnp.int32, (tq, tq), 0)
    col = lax.broadcasted_iota(jnp.int32, (tq, tq), 1)
    strict = col < row
    after = (row > col).astype(BF16)

    def block(kb, carries, masked):
        start = pl.multiple_of(kb * tq, tq)
        st = [{} for _ in range(hp)]
        new = [None] * hp

        def stage_a(h, cs):
            st[h]["y"] = _dot_nt(q_ref[:, cs], k_ref[pl.ds(start, tq), cs]) * neg_scale_log2e

        def stage_b(h, cs):
            y = st[h].pop("y")
            m = jnp.minimum(y, 0.0)
            d = m - y
            t = jnp.log(1.0 + jnp.exp2(m + d)) * LOG2E
            log_keep = m - t
            if masked:
                log_keep = jnp.where(strict, log_keep, 0.0)
            hi = log_keep.astype(BF16)
            st[h].update(hi=hi, lo=(log_keep - hi.astype(F32)).astype(BF16),
                         log_beta=d - t, first=log_keep[:, :1])

        def stage_c(h, cs):
            st[h]["tail"] = _dot(st[h].pop("hi"), after) + _dot(st[h].pop("lo"), after)

        def stage_d(h, cs):
            tail = st[h].pop("tail")
            w = jnp.exp2(st[h].pop("log_beta") + tail + carries[h])
            if masked:
                w = jnp.where(strict, w, 0.0)
            st[h]["w"] = w.astype(BF16)
            new[h] = carries[h] + tail[:, :1] + st[h].pop("first")

        def stage_e(h, cs):
            pv = _dot(st[h].pop("w"), v_ref[pl.ds(start, tq), cs])
            if masked:
                acc_ref[h] = pv
            else:
                acc_ref[h] += pv

        stages = (stage_a, stage_b, stage_c, stage_d, stage_e)
        for slot in range(hp + len(stages) - 1):
            for depth, stage in enumerate(stages):
                h = slot - depth
                if 0 <= h < hp:
                    stage(h, slice(h * dh, (h + 1) * dh))
        return tuple(new)

    carries = block(qi, (jnp.zeros((tq, 1), F32),) * hp, True)
    lax.fori_loop(0, qi, lambda step, cr: block(qi - 1 - step, cr, False), carries)
    for h in range(hp):
        o_ref[:, h * dh:(h + 1) * dh] = acc_ref[h].astype(o_ref.dtype)


def _sb_attention(qkv, heads, tq, hp):
    b, s, w3 = qkv.shape
    dh = w3 // (3 * heads)
    ng = heads // hp
    kern = functools.partial(_sb_kernel, tq=tq, hp=hp, dh=dh,
                             neg_scale_log2e=-(dh ** -0.5) * LOG2E)
    return pl.pallas_call(
        kern,
        grid=(b, ng, s // tq),
        in_specs=[
            pl.BlockSpec((None, tq, hp * dh), lambda bi, g, qi: (bi, qi, g)),
            pl.BlockSpec((None, s, hp * dh), lambda bi, g, qi: (bi, 0, ng + g)),
            pl.BlockSpec((None, s, hp * dh), lambda bi, g, qi: (bi, 0, 2 * ng + g)),
        ],
        out_specs=pl.BlockSpec((None, tq, hp * dh), lambda bi, g, qi: (bi, qi, g)),
        out_shape=jax.ShapeDtypeStruct((b, s, heads * dh), BF16),
        scratch_shapes=[pltpu.VMEM((hp, tq, dh), F32)],
        compiler_params=_params("parallel", "parallel", "arbitrary"),
        name="sb_attention",
    )(qkv, qkv, qkv)


def _hgrn_kernel(lbl_ref, gain_ref, q_ref, f_ref, i_ref, g_ref, o_ref, st_ref, *,
                 layer_j, heads, chunk, tt):
    @pl.when(pl.program_id(1) == 0)
    def _():
        st_ref[...] = jnp.zeros_like(st_ref)

    logits = lbl_ref[...]
    e = jnp.exp(logits - jnp.max(logits, axis=0, keepdims=True))
    sm = e / jnp.sum(e, axis=0, keepdims=True)
    lb = jnp.sum(sm[:layer_j + 1], axis=0, keepdims=True)
    gain = gain_ref[...]

    dk = q_ref.shape[1] // heads
    dv = i_ref.shape[1] // heads
    trow = lax.broadcasted_iota(jnp.int32, (chunk, chunk), 0)
    tcol = lax.broadcasted_iota(jnp.int32, (chunk, chunk), 1)
    causal = tcol <= trow
    lower = causal.astype(BF16)

    def chunk_body(c, _):
        r0 = pl.multiple_of(c * chunk, chunk)
        rows = pl.ds(r0, chunk)
        f = lb + (1.0 - lb) * jax.nn.sigmoid(f_ref[rows, :])
        log_f = jnp.log(f)
        k = 1.0 - f
        qf = _silu(q_ref[rows, :])
        g_cum = _split_dot_left(lower, log_f)
        g_last = g_cum[chunk - 1:chunk, :]
        q_dec = (qf * jnp.exp(g_cum)).astype(BF16)
        k_inv = (k * jnp.exp(-g_cum)).astype(BF16)
        k_end = (k * jnp.exp(g_last - g_cum)).astype(BF16)
        decay = jnp.exp(g_last)
        v_all = i_ref[rows, :]
        gate = _silu(g_ref[rows, :])
        for h in range(heads):
            ks = slice(h * dk, (h + 1) * dk)
            vs = slice(h * dv, (h + 1) * dv)
            v = v_all[:, vs].astype(BF16)
            v_t = v_all[:, vs].T.astype(BF16)
            scores = jnp.where(causal, _dot_nt(q_dec[:, ks], k_inv[:, ks]), 0.0)
            st = st_ref[h]
            o = _dot(scores.astype(BF16), v) + _dot_nt(q_dec[:, ks], st.astype(BF16))
            st_ref[h] = decay[:, ks] * st + _dot(v_t, k_end[:, ks])
            o = o * lax.rsqrt(jnp.mean(o * o, axis=-1, keepdims=True) + NORM_EPS) * gain[:, vs]
            o_ref[rows, vs] = (o * gate[:, vs]).astype(o_ref.dtype)
        return 0

    lax.fori_loop(0, tt // chunk, chunk_body, 0)


def _hgrn2(hg, lb_logits, out_gain, layer_j, heads, tt):
    b, s, w4 = hg.shape
    kw = lb_logits.shape[1]
    vw = out_gain.shape[1]
    assert w4 == 2 * kw + 2 * vw and kw == vw
    nb = w4 // kw
    dk = kw // heads
    dv = vw // heads
    kern = functools.partial(_hgrn_kernel, layer_j=layer_j, heads=heads, chunk=HG_CHUNK, tt=tt)
    col = lambda k: pl.BlockSpec((None, tt, kw), lambda bi, ti: (bi, ti, k))
    assert nb == 4
    return pl.pallas_call(
        kern,
        grid=(b, s // tt),
        in_specs=[
            pl.BlockSpec(lb_logits.shape, lambda bi, ti: (0, 0)),
            pl.BlockSpec((1, vw), lambda bi, ti: (0, 0)),
            col(0), col(1), col(2), col(3),
        ],
        out_specs=pl.BlockSpec((None, tt, vw), lambda bi, ti: (bi, ti, 0)),
        out_shape=jax.ShapeDtypeStruct((b, s, vw), BF16),
        scratch_shapes=[pltpu.VMEM((heads, dv, dk), F32)],
        compiler_params=_params("parallel", "arbitrary"),
        name="hgrn2",
    )(lb_logits, out_gain, hg, hg, hg, hg)


def _residual_matmul_kernel(*refs, n_lhs):
    a_refs = refs[:n_lhs]
    w_refs = refs[n_lhs:2 * n_lhs]
    x_ref, g_ref, o_ref = refs[2 * n_lhs:]
    acc = _dot(a_refs[0][...], w_refs[0][...])
    for a_ref, w_ref in zip(a_refs[1:], w_refs[1:]):
        acc = acc + _dot(a_ref[...], w_ref[...])
    o_ref[...] = x_ref[...] + g_ref[...] * acc


def _residual_matmul(lhs_list, w, x2, mod4, layer, slot, seq, bm, bn, name):
    t, d = x2.shape
    n_lhs = len(lhs_list)
    tps = seq // bm
    in_specs = []
    row0 = 0
    w_specs = []
    for a in lhs_list:
        ka = a.shape[1]
        in_specs.append(pl.BlockSpec((bm, ka), lambda i, j: (i, 0)))
        assert row0 % ka == 0
        w_specs.append(pl.BlockSpec((ka, bn), functools.partial(lambda i, j, r: (r, j), r=row0 // ka)))
        row0 += ka
    assert row0 == w.shape[0]
    in_specs += w_specs
    in_specs += [
        pl.BlockSpec((bm, bn), lambda i, j: (i, j)),
        _mod_spec(layer, slot, d, bn, tps, True),
    ]
    return pl.pallas_call(
        functools.partial(_residual_matmul_kernel, n_lhs=n_lhs),
        grid=(t // bm, d // bn),
        in_specs=in_specs,
        out_specs=pl.BlockSpec((bm, bn), lambda i, j: (i, j)),
        out_shape=jax.ShapeDtypeStruct((t, d), F32),
        compiler_params=_params("parallel", "arbitrary"),
        name=name,
    )(*lhs_list, *([w] * n_lhs), x2, mod4)


def _ffn_up_kernel(x_ref, xh_ref, gain_ref, sc_ref, sh_ref, wg_ref, wv_ref, cwg_ref, cwv_ref,
                   cbg_ref, cbv_ref, o_ref, h_ref, hh_ref, eg0_ref, ev0_ref, eg1_ref, ev1_ref, *,
                   bm, nj, n_tiles, tiles_per_seq):
    n = pl.program_id(0)
    cur = jnp.minimum(n, n_tiles - 1)
    i = cur // nj

    @pl.when(n == 0)
    def _():
        eg1_ref[...] = jnp.zeros_like(eg1_ref)
        ev1_ref[...] = jnp.zeros_like(ev1_ref)

    @pl.when(cur % nj == 0)
    def _():
        _norm_rows(x_ref, h_ref, gain_ref, sc_ref, sh_ref)
        _norm_rows(xh_ref, hh_ref, gain_ref, sc_ref, sh_ref)

    first = (i % tiles_per_seq) == 0

    rb = EPILOGUE_ROWS
    lanes = EPILOGUE_LANES

    bn = o_ref.shape[1]

    def drain_piece(drain, c, row0, nrows):
        cs = slice(c * lanes, (c + 1) * lanes)
        taps = [[jnp.broadcast_to(cw_ref[tap:tap + 1, cs], (rb, lanes))
                 for tap in range(CONV_WIDTH)] for cw_ref in (cwg_ref, cwv_ref)]
        bias = [jnp.broadcast_to(cb_ref[:, cs], (rb, lanes)) for cb_ref in (cbg_ref, cbv_ref)]
        for r0 in range(row0, row0 + nrows, rb):
            conv = []
            for e_ref, tp, bs in zip(drain, taps, bias):
                acc = bs
                for tap in range(CONV_WIDTH):
                    off = HALO_ROWS - (CONV_WIDTH - 1) + tap + r0
                    acc = acc + tp[tap] * e_ref[off:off + rb, cs]
                conv.append(acc)
            o_ref[r0:r0 + rb, cs] = (_silu(conv[0]) * conv[1]).astype(o_ref.dtype)

    def step(fill, drain):
        col_chunks = [(w_ref, e_ref, slice(c0, c0 + MXU_COLUMNS))
                      for w_ref, e_ref in ((wg_ref, fill[0]), (wv_ref, fill[1]))
                      for c0 in range(0, bn, MXU_COLUMNS)]
        assert len(col_chunks) == bn // lanes
        for c, (w_ref, e_ref, ws) in enumerate(col_chunks):
            e_ref[pl.ds(0, HALO_ROWS), ws] = jnp.where(first, 0.0, _dot(hh_ref[...], w_ref[:, ws]))
        for r0 in range(0, bm, FFN_DOT_ROWS):
            for c, (w_ref, e_ref, ws) in enumerate(col_chunks):
                e_ref[pl.ds(HALO_ROWS + r0, FFN_DOT_ROWS), ws] = _dot(
                    h_ref[r0:r0 + FFN_DOT_ROWS, :], w_ref[:, ws])
                drain_piece(drain, c, r0, FFN_DOT_ROWS)

    @pl.when(n % 2 == 0)
    def _():
        step((eg0_ref, ev0_ref), (eg1_ref, ev1_ref))

    @pl.when(n % 2 == 1)
    def _():
        step((eg1_ref, ev1_ref), (eg0_ref, ev0_ref))


def _ffn_up(x2, gain, mod4, layer, w_up, conv_w, conv_b, seq, bm, bn):
    t, d = x2.shape
    f = w_up.shape[1] // 2
    nj = f // bn
    tps = seq // bm
    hb = bm // HALO_ROWS
    n_tiles = (t // bm) * nj
    depth = conv_w.shape[0]
    conv_b3 = conv_b.reshape(depth, 1, 2 * f)
    kern = functools.partial(_ffn_up_kernel, bm=bm, nj=nj, n_tiles=n_tiles, tiles_per_seq=tps)
    cur_i = lambda n: jnp.minimum(n, n_tiles - 1) // nj
    cur_j = lambda n: jnp.minimum(n, n_tiles - 1) % nj
    prev_i = lambda n: jnp.maximum(n - 1, 0) // nj
    prev_j = lambda n: jnp.maximum(n - 1, 0) % nj
    mod_spec = lambda slot: pl.BlockSpec((None, None, 1, d),
                                         lambda n: (layer, cur_i(n) // tps, 0, slot))
    e_buf = pltpu.VMEM((bm + HALO_ROWS, bn), F32)
    return pl.pallas_call(
        kern,
        grid=(n_tiles + 1,),
        in_specs=[
            pl.BlockSpec((bm, d), lambda n: (cur_i(n), 0)),
            pl.BlockSpec((HALO_ROWS, d), lambda n: (jnp.maximum(cur_i(n) * hb - 1, 0), 0)),
            pl.BlockSpec((1, d), lambda n: (0, 0)),
            mod_spec(4),
            mod_spec(3),
            pl.BlockSpec((d, bn), lambda n: (0, cur_j(n))),
            pl.BlockSpec((d, bn), lambda n: (0, nj + cur_j(n))),
            pl.BlockSpec((None, CONV_WIDTH, bn), lambda n: (layer, 0, prev_j(n))),
            pl.BlockSpec((None, CONV_WIDTH, bn), lambda n: (layer, 0, nj + prev_j(n))),
            pl.BlockSpec((None, 1, bn), lambda n: (layer, 0, prev_j(n))),
            pl.BlockSpec((None, 1, bn), lambda n: (layer, 0, nj + prev_j(n))),
        ],
        out_specs=pl.BlockSpec((bm, bn), lambda n: (prev_i(n), prev_j(n))),
        out_shape=jax.ShapeDtypeStruct((t, f), BF16),
        scratch_shapes=[
            pltpu.VMEM((bm, d), BF16),
            pltpu.VMEM((HALO_ROWS, d), BF16),
            e_buf, e_buf, e_buf, e_buf,
        ],
        compiler_params=_params("arbitrary"),
        name="ffn_up",
    )(x2, x2, gain, mod4, mod4, w_up, w_up, conv_w, conv_w, conv_b3, conv_b3)


def _gelu(x):
    return 0.5 * x * (1.0 + lax.erf(x * (2.0 ** -0.5)))


def _sg_in_kernel(x_ref, gain_ref, sc_ref, sh_ref, w_ref, vg_ref, vb_ref, u_ref, v_ref, h_ref):
    j = pl.program_id(1)

    @pl.when(j == 0)
    def _():
        _norm_rows(x_ref, h_ref, gain_ref, sc_ref, sh_ref)

    z = _gelu(_dot(h_ref[...], w_ref[...]))

    @pl.when(j == 0)
    def _():
        u_ref[...] = z.astype(u_ref.dtype)

    @pl.when(j == 1)
    def _():
        zc = z - jnp.mean(z, axis=-1, keepdims=True)
        y = zc * lax.rsqrt(jnp.mean(zc * zc, axis=-1, keepdims=True) + NORM_EPS)
        v_ref[...] = (y * vg_ref[...] + vb_ref[...]).astype(v_ref.dtype)


def _sg_in(x2, gain, mod4, layer, w_in, v_gain, v_bias, seq, bm):
    t, d = x2.shape
    width = w_in.shape[1] // 2
    tps = seq // bm
    out = jax.ShapeDtypeStruct((t, width), BF16)
    return pl.pallas_call(
        _sg_in_kernel,
        grid=(t // bm, 2),
        in_specs=[
            pl.BlockSpec((bm, d), lambda i, j: (i, 0)),
            pl.BlockSpec((1, d), lambda i, j: (0, 0)),
            _mod_spec(layer, 1, d, d, tps, False),
            _mod_spec(layer, 0, d, d, tps, False),
            pl.BlockSpec((d, width), lambda i, j: (0, j)),
            pl.BlockSpec((1, width), lambda i, j: (0, 0)),
            pl.BlockSpec((1, width), lambda i, j: (0, 0)),
        ],
        out_specs=[pl.BlockSpec((bm, width), lambda i, j: (i, 0)),
                   pl.BlockSpec((bm, width), lambda i, j: (i, 0))],
        out_shape=[out, out],
        scratch_shapes=[pltpu.VMEM((bm, d), BF16)],
        compiler_params=_params("parallel", "arbitrary"),
        name="sg_in",
    )(x2, gain, mod4, mod4, w_in, v_gain, v_bias)


def _sg_out_kernel(u_ref, v_ref, wp_ref, bp_ref, w_ref, x_ref, g_ref, o_ref, m_ref, *,
                   bm, groups, chunk):
    @pl.when(pl.program_id(1) == 0)
    def _():
        gw = u_ref.shape[1] // groups
        row = lax.broadcasted_iota(jnp.int32, (chunk, chunk), 0)
        col = lax.broadcasted_iota(jnp.int32, (chunk, chunk), 1)
        causal = col <= row
        for g in range(groups):
            w_pos = jnp.where(causal, wp_ref[g], 0.0).astype(BF16)
            bias = bp_ref[:, g:g + 1]
            cs = slice(g * gw, (g + 1) * gw)
            for r in range(bm // chunk):
                rs = slice(r * chunk, (r + 1) * chunk)
                mixed = _dot(w_pos, v_ref[rs, cs]) + bias
                m_ref[rs, cs] = (u_ref[rs, cs].astype(F32) * mixed).astype(BF16)

    o_ref[...] = x_ref[...] + g_ref[...] * _dot(m_ref[...], w_ref[...])


def _sg_out(u, v, w_pos, b_pos_t, w_out, x2, mod4, layer, seq, bm, bn):
    t, d = x2.shape
    width = u.shape[1]
    tps = seq // bm
    kern = functools.partial(_sg_out_kernel, bm=bm, groups=SG_GROUPS, chunk=SG_CHUNK)
    return pl.pallas_call(
        kern,
        grid=(t // bm, d // bn),
        in_specs=[
            pl.BlockSpec((bm, width), lambda i, j: (i, 0)),
            pl.BlockSpec((bm, width), lambda i, j: (i, 0)),
            pl.BlockSpec(w_pos.shape, lambda i, j: (0, 0, 0)),
            pl.BlockSpec(b_pos_t.shape, lambda i, j: (0, 0)),
            pl.BlockSpec((width, bn), lambda i, j: (0, j)),
            pl.BlockSpec((bm, bn), lambda i, j: (i, j)),
            _mod_spec(layer, 2, d, bn, tps, True),
        ],
        out_specs=pl.BlockSpec((bm, bn), lambda i, j: (i, j)),
        out_shape=jax.ShapeDtypeStruct((t, d), F32),
        scratch_shapes=[pltpu.VMEM((bm, width), BF16)],
        compiler_params=_params("parallel", "arbitrary"),
        name="sg_out",
    )(u, v, w_pos, b_pos_t, w_out, x2, mod4)


def _final_norm_kernel(x_ref, gain_ref, o_ref):
    x = x_ref[...]
    o_ref[...] = x * lax.rsqrt(jnp.mean(x * x, axis=-1, keepdims=True) + NORM_EPS) * gain_ref[...]


def _final_norm(x2, gain, bm):
    t, d = x2.shape
    return pl.pallas_call(
        _final_norm_kernel,
        grid=(t // bm,),
        in_specs=[pl.BlockSpec((bm, d), lambda i: (i, 0)),
                  pl.BlockSpec((1, d), lambda i: (0, 0))],
        out_specs=pl.BlockSpec((bm, d), lambda i: (i, 0)),
        out_shape=jax.ShapeDtypeStruct((t, d), F32),
        compiler_params=_params("parallel"),
        name="final_norm",
    )(x2, gain)


def kernel(x, c, ada_w, ada_b, mix_norm, ffn_norm, par_w_in, par_w_out, hg_lb_logits, hg_out_norm, sg_w_in, sg_v_gain, sg_v_bias, sg_w_pos, sg_b_pos, sg_w_out, ffn_up, ffn_conv_w, ffn_conv_b, ffn_down, final_norm):
    b, s, d = x.shape
    depth = ada_w.shape[0]
    t = b * s
    sb_w = d // 2
    bm = min(ROW_TILE, s)
    bm_sg = min(SG_ROW_TILE, s)

    c_pad = jnp.pad(c, ((0, ADA_PAD_ROWS - b), (0, 0)))
    mod = _ada_mod(c_pad, ada_w, ada_b)
    mod4 = mod.reshape(depth, ADA_PAD_ROWS, 1, 6 * d)

    x2 = x.reshape(t, d)
    for layer in range(depth):
        j = layer // 2
        gain1 = mix_norm[layer].reshape(1, d)
        if layer % 2 == 0:
            qkv, hg = _par_proj(x2, gain1, mod4, layer, par_w_in[j].astype(BF16), 3 * sb_w,
                                s, bm, 1024)
            o_sb = _sb_attention(qkv.reshape(b, s, 3 * sb_w), SB_HEADS, min(SB_QUERY_TILE, s),
                                 SB_HEADS_PER_STEP)
            o_hg = _hgrn2(hg.reshape(b, s, hg.shape[1]), hg_lb_logits,
                          hg_out_norm[j].reshape(1, -1), j, HG_HEADS, min(HG_TIME_TILE, s))
            x2 = _residual_matmul([o_sb.reshape(t, sb_w), o_hg.reshape(t, -1)],
                                  par_w_out[j].astype(BF16), x2, mod4, layer, 2, s, bm, 1024,
                                  "par_out")
        else:
            u, v = _sg_in(x2, gain1, mod4, layer, sg_w_in[j].astype(BF16),
                          sg_v_gain[j].reshape(1, -1), sg_v_bias[j].reshape(1, -1), s, bm_sg)
            x2 = _sg_out(u, v, sg_w_pos[j], sg_b_pos[j].T, sg_w_out[j].astype(BF16), x2, mod4,
                         layer, s, bm_sg, 1024)
        act = _ffn_up(x2, ffn_norm[layer].reshape(1, d), mod4, layer, ffn_up[layer].astype(BF16),
                      ffn_conv_w, ffn_conv_b, s, bm, 512)
        x2 = _residual_matmul([act], ffn_down[layer].astype(BF16), x2, mod4, layer, 5, s, bm, 512,
                              "ffn_down")
    out = _final_norm(x2, final_norm.reshape(1, d), bm_sg)
    return out.reshape(b, s, d)
```

```python
import functools

import jax
import jax.numpy as jnp
from jax import lax
from jax.experimental import pallas as pl
from jax.experimental.pallas import tpu as pltpu

F32 = jnp.float32
BF16 = jnp.bfloat16

NORM_EPS = 1e-6
SB_HEADS = 8
HG_HEADS = 8
HG_CHUNK = 64
SG_GROUPS = 8
SG_CHUNK = 128
CONV_WIDTH = 3
HALO_ROWS = 8

LOG2E = 1.4426950408889634
NORM_ROW_BLOCK = 64
SB_HEADS_PER_STEP = 8
EPILOGUE_ROWS = 16
EPILOGUE_LANES = 128
FFN_DOT_ROWS = 1024
MXU_COLUMNS = 256
ROW_TILE = 1024
SG_ROW_TILE = 512
SB_QUERY_TILE = 256
HG_TIME_TILE = 512

V7X_VMEM_LIMIT_BYTES =56 * 1024 * 1024
ADA_PAD_ROWS = 16


def _params(*sem):
    return pltpu.CompilerParams(dimension_semantics=sem, vmem_limit_bytes=V7X_VMEM_LIMIT_BYTES)


def _silu(x):
    return x * jax.nn.sigmoid(x)


def _dot(a, b):
    return jnp.dot(a, b, preferred_element_type=F32)


def _dot_nt(a, b):
    return lax.dot_general(a, b, (((1,), (1,)), ((), ())), preferred_element_type=F32)


def _split_dot(a_f32, b_bf16):
    hi = a_f32.astype(BF16)
    lo = (a_f32 - hi.astype(F32)).astype(BF16)
    return _dot(hi, b_bf16) + _dot(lo, b_bf16)


def _split_dot_left(b_bf16, a_f32):
    hi = a_f32.astype(BF16)
    lo = (a_f32 - hi.astype(F32)).astype(BF16)
    return _dot(b_bf16, hi) + _dot(b_bf16, lo)


def _modulated_rms_norm(x, gain_scale, shift):
    inv = lax.rsqrt(jnp.mean(x * x, axis=-1, keepdims=True) + NORM_EPS)
    return x * inv * gain_scale + shift


def _norm_rows(x_ref, h_ref, gain_ref, sc_ref, sh_ref):
    rows = x_ref.shape[0]
    rb = min(NORM_ROW_BLOCK, rows)
    gain_scale = gain_ref[...] * (1.0 + sc_ref[...])
    shift = sh_ref[...]

    def body(r, _):
        rs = pl.ds(pl.multiple_of(r * rb, rb), rb)
        h_ref[rs, :] = _modulated_rms_norm(x_ref[rs, :], gain_scale, shift).astype(h_ref.dtype)
        return 0

    lax.fori_loop(0, rows // rb, body, 0, unroll=min(2, rows // rb))


def _ada_kernel(c_ref, w_ref, b_ref, o_ref):
    cond = _silu(c_ref[...]).astype(BF16)
    o_ref[...] = _dot(cond, w_ref[...].astype(BF16)) + b_ref[...]


def _ada_mod(c_pad, ada_w, ada_b):
    depth, d, n = ada_w.shape
    tn = 1024
    return pl.pallas_call(
        _ada_kernel,
        grid=(depth, n // tn),
        in_specs=[
            pl.BlockSpec((ADA_PAD_ROWS, d), lambda l, j: (0, 0)),
            pl.BlockSpec((None, d, tn), lambda l, j: (l, 0, j)),
            pl.BlockSpec((None, 1, tn), lambda l, j: (l, 0, j)),
        ],
        out_specs=pl.BlockSpec((None, ADA_PAD_ROWS, tn), lambda l, j: (l, 0, j)),
        out_shape=jax.ShapeDtypeStruct((depth, ADA_PAD_ROWS, n), F32),
        compiler_params=_params("parallel", "arbitrary"),
        name="ada_mod",
    )(c_pad, ada_w, ada_b.reshape(depth, 1, n))


def _mod_spec(layer, slot, d, bn, tiles_per_seq, with_j):
    nb = d // bn
    if with_j:
        return pl.BlockSpec((None, None, 1, bn),
                            lambda i, j: (layer, i // tiles_per_seq, 0, slot * nb + j))
    return pl.BlockSpec((None, None, 1, bn),
                        lambda i, j: (layer, i // tiles_per_seq, 0, slot * nb))


def _par_proj_kernel(x_ref, gain_ref, sc_ref, sh_ref, w_ref, qkv_ref, hg_ref, h_ref, *, n_sb):
    j = pl.program_id(1)

    @pl.when(j == 0)
    def _():
        _norm_rows(x_ref, h_ref, gain_ref, sc_ref, sh_ref)

    @pl.when(j < n_sb)
    def _():
        qkv_ref[...] = _dot(h_ref[...], w_ref[...]).astype(qkv_ref.dtype)

    @pl.when(j >= n_sb)
    def _():
        hg_ref[...] = _dot(h_ref[...], w_ref[...])


def _par_proj(x2, gain, mod4, layer, w, sb_cols, seq, bm, bn):
    t, d = x2.shape
    n = w.shape[1]
    tps = seq // bm
    n_sb = sb_cols // bn
    return pl.pallas_call(
        functools.partial(_par_proj_kernel, n_sb=n_sb),
        grid=(t // bm, n // bn),
        in_specs=[
            pl.BlockSpec((bm, d), lambda i, j: (i, 0)),
            pl.BlockSpec((1, d), lambda i, j: (0, 0)),
            _mod_spec(layer, 1, d, d, tps, False),
            _mod_spec(layer, 0, d, d, tps, False),
            pl.BlockSpec((d, bn), lambda i, j: (0, j)),
        ],
        out_specs=[
            pl.BlockSpec((bm, bn), lambda i, j: (i, jnp.minimum(j, n_sb - 1))),
            pl.BlockSpec((bm, bn), lambda i, j: (i, jnp.maximum(j - n_sb, 0))),
        ],
        out_shape=[jax.ShapeDtypeStruct((t, sb_cols), BF16),
                   jax.ShapeDtypeStruct((t, n - sb_cols), F32)],
        scratch_shapes=[pltpu.VMEM((bm, d), BF16)],
        compiler_params=_params("parallel", "arbitrary"),
        name="par_proj",
    )(x2, gain, mod4, mod4, w)


def _sb_kernel(q_ref, k_ref, v_ref, o_ref, acc_ref, *, tq, hp, dh, neg_scale_log2e):
    qi = pl.program_id(2)
    row = lax.broadcasted_iota(jnp.int32, (tq, tq), 0)
    col = lax.broadcasted_iota(jnp.int32, (tq, tq), 1)
    strict = col < row
    after = (row > col).astype(BF16)

    def block(kb, carries, masked):
        start = pl.multiple_of(kb * tq, tq)
        st = [{} for _ in range(hp)]
        new = [None] * hp

        def stage_a(h, cs):
            st[h]["y"] = _dot_nt(q_ref[:, cs], k_ref[pl.ds(start, tq), cs]) * neg_scale_log2e

        def stage_b(h, cs):
            y = st[h].pop("y")
            m = jnp.minimum(y, 0.0)
            d = m - y
            t = jnp.log(1.0 + jnp.exp2(m + d)) * LOG2E
            log_keep = m - t
            if masked:
                log_keep = jnp.where(strict, log_keep, 0.0)
            st[h].update(keep=log_keep.astype(BF16), log_beta=d - t,
                         first=log_keep[:, :1].astype(BF16).astype(F32))

        def stage_c(h, cs):
            st[h]["tail"] = _dot(st[h].pop("keep"), after)

        def stage_d(h, cs):
            tail = st[h].pop("tail")
            w = jnp.exp2(st[h].pop("log_beta") + tail + carries[h])
            if masked:
                w = jnp.where(strict, w, 0.0)
            st[h]["w"] = w.astype(BF16)
            new[h] = carries[h] + tail[:, :1] + st[h].pop("first")

        def stage_e(h, cs):
            pv = _dot(st[h].pop("w"), v_ref[pl.ds(start, tq), cs])
            if masked:
                acc_ref[h] = pv
            else:
                acc_ref[h] += pv

        stages = (stage_a, stage_b, stage_c, stage_d, stage_e)
        for slot in range(hp + len(stages) - 1):
            for depth, stage in enumerate(stages):
                h = slot - depth
                if 0 <= h < hp:
                    stage(h, slice(h * dh, (h + 1) * dh))
        return tuple(new)

    carries = block(qi, (jnp.zeros((tq, 1), F32),) * hp, True)
    lax.fori_loop(0, qi, lambda step, cr: block(qi - 1 - step, cr, False), carries)
    for h in range(hp):
        o_ref[:, h * dh:(h + 1) * dh] = acc_ref[h].astype(o_ref.dtype)


def _sb_attention(qkv, heads, tq, hp):
    b, s, w3 = qkv.shape
    dh = w3 // (3 * heads)
    ng = heads // hp
    kern = functools.partial(_sb_kernel, tq=tq, hp=hp, dh=dh,
                             neg_scale_log2e=-(dh ** -0.5) * LOG2E)
    return pl.pallas_call(
        kern,
        grid=(b, ng, s // tq),
        in_specs=[
            pl.BlockSpec((None, tq, hp * dh), lambda bi, g, qi: (bi, qi, g)),
            pl.BlockSpec((None, s, hp * dh), lambda bi, g, qi: (bi, 0, ng + g)),
            pl.BlockSpec((None, s, hp * dh), lambda bi, g, qi: (bi, 0, 2 * ng + g)),
        ],
        out_specs=pl.BlockSpec((None, tq, hp * dh), lambda bi, g, qi: (bi, qi, g)),
        out_shape=jax.ShapeDtypeStruct((b, s, heads * dh), BF16),
        scratch_shapes=[pltpu.VMEM((hp, tq, dh), F32)],
        compiler_params=_params("parallel", "parallel", "arbitrary"),
        name="sb_attention",
    )(qkv, qkv, qkv)


def _hgrn_kernel(lbl_ref, gain_ref, q_ref, f_ref, i_ref, g_ref, o_ref, st_ref, *,
                 layer_j, heads, chunk, tt):
    @pl.when(pl.program_id(1) == 0)
    def _():
        st_ref[...] = jnp.zeros_like(st_ref)

    logits = lbl_ref[...]
    e = jnp.exp(logits - jnp.max(logits, axis=0, keepdims=True))
    sm = e / jnp.sum(e, axis=0, keepdims=True)
    lb = jnp.sum(sm[:layer_j + 1], axis=0, keepdims=True)
    gain = gain_ref[...]

    dk = q_ref.shape[1] // heads
    dv = i_ref.shape[1] // heads
    trow = lax.broadcasted_iota(jnp.int32, (chunk, chunk), 0)
    tcol = lax.broadcasted_iota(jnp.int32, (chunk, chunk), 1)
    causal = tcol <= trow
    lower = causal.astype(BF16)

    def chunk_body(c, _):
        r0 = pl.multiple_of(c * chunk, chunk)
        rows = pl.ds(r0, chunk)
        f = lb + (1.0 - lb) * jax.nn.sigmoid(f_ref[rows, :])
        log_f = jnp.log(f)
        k = 1.0 - f
        qf = _silu(q_ref[rows, :])
        g_cum = _split_dot_left(lower, log_f)
        g_last = g_cum[chunk - 1:chunk, :]
        q_dec = (qf * jnp.exp(g_cum)).astype(BF16)
        k_inv = (k * jnp.exp(-g_cum)).astype(BF16)
        k_end = (k * jnp.exp(g_last - g_cum)).astype(BF16)
        decay = jnp.exp(g_last)
        v_all = i_ref[rows, :]
        gate = _silu(g_ref[rows, :])
        for h in range(heads):
            ks = slice(h * dk, (h + 1) * dk)
            vs = slice(h * dv, (h + 1) * dv)
            v = v_all[:, vs].astype(BF16)
            v_t = v_all[:, vs].T.astype(BF16)
            scores = jnp.where(causal, _dot_nt(q_dec[:, ks], k_inv[:, ks]), 0.0)
            st = st_ref[h]
            o = _dot(scores.astype(BF16), v) + _dot_nt(q_dec[:, ks], st.astype(BF16))
            st_ref[h] = decay[:, ks] * st + _dot(v_t, k_end[:, ks])
            o = o * lax.rsqrt(jnp.mean(o * o, axis=-1, keepdims=True) + NORM_EPS) * gain[:, vs]
            o_ref[rows, vs] = (o * gate[:, vs]).astype(o_ref.dtype)
        return 0

    lax.fori_loop(0, tt // chunk, chunk_body, 0)


def _hgrn2(hg, lb_logits, out_gain, layer_j, heads, tt):
    b, s, w4 = hg.shape
    kw = lb_logits.shape[1]
    vw = out_gain.shape[1]
    assert w4 == 2 * kw + 2 * vw and kw == vw
    nb = w4 // kw
    dk = kw // heads
    dv = vw // heads
    kern = functools.partial(_hgrn_kernel, layer_j=layer_j, heads=heads, chunk=HG_CHUNK, tt=tt)
    col = lambda k: pl.BlockSpec((None, tt, kw), lambda bi, ti: (bi, ti, k))
    assert nb == 4
    return pl.pallas_call(
        kern,
        grid=(b, s // tt),
        in_specs=[
            pl.BlockSpec(lb_logits.shape, lambda bi, ti: (0, 0)),
            pl.BlockSpec((1, vw), lambda bi, ti: (0, 0)),
            col(0), col(1), col(2), col(3),
        ],
        out_specs=pl.BlockSpec((None, tt, vw), lambda bi, ti: (bi, ti, 0)),
        out_shape=jax.ShapeDtypeStruct((b, s, vw), BF16),
        scratch_shapes=[pltpu.VMEM((heads, dv, dk), F32)],
        compiler_params=_params("parallel", "arbitrary"),
        name="hgrn2",
    )(lb_logits, out_gain, hg, hg, hg, hg)


def _residual_matmul_kernel(*refs, n_lhs):
    a_refs = refs[:n_lhs]
    w_refs = refs[n_lhs:2 * n_lhs]
    x_ref, g_ref, o_ref = refs[2 * n_lhs:]
    acc = _dot(a_refs[0][...], w_refs[0][...])
    for a_ref, w_ref in zip(a_refs[1:], w_refs[1:]):
        acc = acc + _dot(a_ref[...], w_ref[...])
    o_ref[...] = x_ref[...] + g_ref[...] * acc


def _residual_matmul(lhs_list, w, w_index, x2, mod4, layer, slot, seq, bm, bn, name):
    t, d = x2.shape
    n_lhs = len(lhs_list)
    tps = seq // bm
    in_specs = []
    row0 = 0
    w_specs = []
    for a in lhs_list:
        ka = a.shape[1]
        in_specs.append(pl.BlockSpec((bm, ka), lambda i, j: (i, 0)))
        assert row0 % ka == 0
        w_specs.append(pl.BlockSpec((None, ka, bn), functools.partial(
            lambda i, j, r: (w_index, r, j), r=row0 // ka)))
        row0 += ka
    assert row0 == w.shape[1]
    in_specs += w_specs
    in_specs += [
        pl.BlockSpec((bm, bn), lambda i, j: (i, j)),
        _mod_spec(layer, slot, d, bn, tps, True),
    ]
    return pl.pallas_call(
        functools.partial(_residual_matmul_kernel, n_lhs=n_lhs),
        grid=(t // bm, d // bn),
        in_specs=in_specs,
        out_specs=pl.BlockSpec((bm, bn), lambda i, j: (i, j)),
        out_shape=jax.ShapeDtypeStruct((t, d), F32),
        compiler_params=_params("parallel", "arbitrary"),
        name=name,
    )(*lhs_list, *([w] * n_lhs), x2, mod4)


def _ffn_up_kernel(x_ref, xh_ref, gain_ref, sc_ref, sh_ref, wg_ref, wv_ref, cwg_ref, cwv_ref,
                   cbg_ref, cbv_ref, o_ref, h_ref, hh_ref, eg0_ref, ev0_ref, eg1_ref, ev1_ref, *,
                   bm, nj, n_tiles, tiles_per_seq):
    n = pl.program_id(0)
    cur = jnp.minimum(n, n_tiles - 1)
    i = cur // nj

    @pl.when(n == 0)
    def _():
        eg1_ref[...] = jnp.zeros_like(eg1_ref)
        ev1_ref[...] = jnp.zeros_like(ev1_ref)

    @pl.when(cur % nj == 0)
    def _():
        _norm_rows(x_ref, h_ref, gain_ref, sc_ref, sh_ref)
        _norm_rows(xh_ref, hh_ref, gain_ref, sc_ref, sh_ref)

    first = (i % tiles_per_seq) == 0

    rb = EPILOGUE_ROWS
    lanes = EPILOGUE_LANES

    bn = o_ref.shape[1]

    def drain_piece(drain, c, row0, nrows):
        cs = slice(c * lanes, (c + 1) * lanes)
        taps = [[jnp.broadcast_to(cw_ref[tap:tap + 1, cs], (rb, lanes))
                 for tap in range(CONV_WIDTH)] for cw_ref in (cwg_ref, cwv_ref)]
        bias = [jnp.broadcast_to(cb_ref[:, cs], (rb, lanes)) for cb_ref in (cbg_ref, cbv_ref)]
        for r0 in range(row0, row0 + nrows, rb):
            conv = []
            for e_ref, tp, bs in zip(drain, taps, bias):
                acc = bs
                for tap in range(CONV_WIDTH):
                    off = HALO_ROWS - (CONV_WIDTH - 1) + tap + r0
                    acc = acc + tp[tap] * e_ref[off:off + rb, cs]
                conv.append(acc)
            o_ref[r0:r0 + rb, cs] = (_silu(conv[0]) * conv[1]).astype(o_ref.dtype)

    def step(fill, drain):
        col_chunks = [(w_ref, e_ref, slice(c0, c0 + MXU_COLUMNS))
                      for w_ref, e_ref in ((wg_ref, fill[0]), (wv_ref, fill[1]))
                      for c0 in range(0, bn, MXU_COLUMNS)]
        assert len(col_chunks) == bn // lanes
        for c, (w_ref, e_ref, ws) in enumerate(col_chunks):
            e_ref[pl.ds(0, HALO_ROWS), ws] = jnp.where(first, 0.0, _dot(hh_ref[...], w_ref[:, ws]))
        dr = min(FFN_DOT_ROWS, bm)
        for r0 in range(0, bm, dr):
            for c, (w_ref, e_ref, ws) in enumerate(col_chunks):
                e_ref[pl.ds(HALO_ROWS + r0, dr), ws] = _dot(h_ref[r0:r0 + dr, :], w_ref[:, ws])
                drain_piece(drain, c, r0, dr)

    @pl.when(n % 2 == 0)
    def _():
        step((eg0_ref, ev0_ref), (eg1_ref, ev1_ref))

    @pl.when(n % 2 == 1)
    def _():
        step((eg1_ref, ev1_ref), (eg0_ref, ev0_ref))


def _ffn_up(x2, gain, mod4, layer, w_up, conv_w, conv_b, seq, bm, bn):
    t, d = x2.shape
    f = w_up.shape[2] // 2
    nj = f // bn
    tps = seq // bm
    hb = bm // HALO_ROWS
    n_tiles = (t // bm) * nj
    depth = conv_w.shape[0]
    conv_b3 = conv_b.reshape(depth, 1, 2 * f)
    kern = functools.partial(_ffn_up_kernel, bm=bm, nj=nj, n_tiles=n_tiles, tiles_per_seq=tps)
    cur_i = lambda n: jnp.minimum(n, n_tiles - 1) // nj
    cur_j = lambda n: jnp.minimum(n, n_tiles - 1) % nj
    prev_i = lambda n: jnp.maximum(n - 1, 0) // nj
    prev_j = lambda n: jnp.maximum(n - 1, 0) % nj
    mod_spec = lambda slot: pl.BlockSpec((None, None, 1, d),
                                         lambda n: (layer, cur_i(n) // tps, 0, slot))
    e_buf = pltpu.VMEM((bm + HALO_ROWS, bn), F32)
    return pl.pallas_call(
        kern,
        grid=(n_tiles + 1,),
        in_specs=[
            pl.BlockSpec((bm, d), lambda n: (cur_i(n), 0)),
            pl.BlockSpec((HALO_ROWS, d), lambda n: (jnp.maximum(cur_i(n) * hb - 1, 0), 0)),
            pl.BlockSpec((1, d), lambda n: (0, 0)),
            mod_spec(4),
            mod_spec(3),
            pl.BlockSpec((None, d, bn), lambda n: (layer, 0, cur_j(n))),
            pl.BlockSpec((None, d, bn), lambda n: (layer, 0, nj + cur_j(n))),
            pl.BlockSpec((None, CONV_WIDTH, bn), lambda n: (layer, 0, prev_j(n))),
            pl.BlockSpec((None, CONV_WIDTH, bn), lambda n: (layer, 0, nj + prev_j(n))),
            pl.BlockSpec((None, 1, bn), lambda n: (layer, 0, prev_j(n))),
            pl.BlockSpec((None, 1, bn), lambda n: (layer, 0, nj + prev_j(n))),
        ],
        out_specs=pl.BlockSpec((bm, bn), lambda n: (prev_i(n), prev_j(n))),
        out_shape=jax.ShapeDtypeStruct((t, f), BF16),
        scratch_shapes=[
            pltpu.VMEM((bm, d), BF16),
            pltpu.VMEM((HALO_ROWS, d), BF16),
            e_buf, e_buf, e_buf, e_buf,
        ],
        compiler_params=_params("arbitrary"),
        name="ffn_up",
    )(x2, x2, gain, mod4, mod4, w_up, w_up, conv_w, conv_w, conv_b3, conv_b3)


def _gelu(x):
    return 0.5 * x * (1.0 + lax.erf(x * (2.0 ** -0.5)))


def _sg_in_kernel(x_ref, gain_ref, sc_ref, sh_ref, w_ref, vg_ref, vb_ref, u_ref, v_ref, h_ref):
    j = pl.program_id(1)

    @pl.when(j == 0)
    def _():
        _norm_rows(x_ref, h_ref, gain_ref, sc_ref, sh_ref)

    z = _gelu(_dot(h_ref[...], w_ref[...]))

    @pl.when(j == 0)
    def _():
        u_ref[...] = z.astype(u_ref.dtype)

    @pl.when(j == 1)
    def _():
        zc = z - jnp.mean(z, axis=-1, keepdims=True)
        y = zc * lax.rsqrt(jnp.mean(zc * zc, axis=-1, keepdims=True) + NORM_EPS)
        v_ref[...] = (y * vg_ref[...] + vb_ref[...]).astype(v_ref.dtype)


def _sg_in(x2, gain, mod4, layer, w_in, v_gain, v_bias, seq, bm):
    t, d = x2.shape
    width = w_in.shape[1] // 2
    tps = seq // bm
    out = jax.ShapeDtypeStruct((t, width), BF16)
    return pl.pallas_call(
        _sg_in_kernel,
        grid=(t // bm, 2),
        in_specs=[
            pl.BlockSpec((bm, d), lambda i, j: (i, 0)),
            pl.BlockSpec((1, d), lambda i, j: (0, 0)),
            _mod_spec(layer, 1, d, d, tps, False),
            _mod_spec(layer, 0, d, d, tps, False),
            pl.BlockSpec((d, width), lambda i, j: (0, j)),
            pl.BlockSpec((1, width), lambda i, j: (0, 0)),
            pl.BlockSpec((1, width), lambda i, j: (0, 0)),
        ],
        out_specs=[pl.BlockSpec((bm, width), lambda i, j: (i, 0)),
                   pl.BlockSpec((bm, width), lambda i, j: (i, 0))],
        out_shape=[out, out],
        scratch_shapes=[pltpu.VMEM((bm, d), BF16)],
        compiler_params=_params("parallel", "arbitrary"),
        name="sg_in",
    )(x2, gain, mod4, mod4, w_in, v_gain, v_bias)


def _sg_out_kernel(u_ref, v_ref, wp_ref, bp_ref, w_ref, x_ref, g_ref, o_ref, m_ref, *,
                   bm, groups, chunk):
    @pl.when(pl.program_id(1) == 0)
    def _():
        gw = u_ref.shape[1] // groups
        row = lax.broadcasted_iota(jnp.int32, (chunk, chunk), 0)
        col = lax.broadcasted_iota(jnp.int32, (chunk, chunk), 1)
        causal = col <= row
        for g in range(groups):
            w_pos = jnp.where(causal, wp_ref[g], 0.0).astype(BF16)
            bias = bp_ref[:, g:g + 1]
            cs = slice(g * gw, (g + 1) * gw)
            for r in range(bm // chunk):
                rs = slice(r * chunk, (r + 1) * chunk)
                mixed = _dot(w_pos, v_ref[rs, cs]) + bias
                m_ref[rs, cs] = (u_ref[rs, cs].astype(F32) * mixed).astype(BF16)

    o_ref[...] = x_ref[...] + g_ref[...] * _dot(m_ref[...], w_ref[...])


def _sg_out(u, v, w_pos, b_pos_t, w_out, x2, mod4, layer, seq, bm, bn):
    t, d = x2.shape
    width = u.shape[1]
    tps = seq // bm
    kern = functools.partial(_sg_out_kernel, bm=bm, groups=SG_GROUPS, chunk=SG_CHUNK)
    return pl.pallas_call(
        kern,
        grid=(t // bm, d // bn),
        in_specs=[
            pl.BlockSpec((bm, width), lambda i, j: (i, 0)),
            pl.BlockSpec((bm, width), lambda i, j: (i, 0)),
            pl.BlockSpec(w_pos.shape, lambda i, j: (0, 0, 0)),
            pl.BlockSpec(b_pos_t.shape, lambda i, j: (0, 0)),
            pl.BlockSpec((width, bn), lambda i, j: (0, j)),
            pl.BlockSpec((bm, bn), lambda i, j: (i, j)),
            _mod_spec(layer, 2, d, bn, tps, True),
        ],
        out_specs=pl.BlockSpec((bm, bn), lambda i, j: (i, j)),
        out_shape=jax.ShapeDtypeStruct((t, d), F32),
        scratch_shapes=[pltpu.VMEM((bm, width), BF16)],
        compiler_params=_params("parallel", "arbitrary"),
        name="sg_out",
    )(u, v, w_pos, b_pos_t, w_out, x2, mod4)


def _final_norm_kernel(x_ref, gain_ref, o_ref):
    x = x_ref[...]
    o_ref[...] = x * lax.rsqrt(jnp.mean(x * x, axis=-1, keepdims=True) + NORM_EPS) * gain_ref[...]


def _final_norm(x2, gain, bm):
    t, d = x2.shape
    return pl.pallas_call(
        _final_norm_kernel,
        grid=(t // bm,),
        in_specs=[pl.BlockSpec((bm, d), lambda i: (i, 0)),
                  pl.BlockSpec((1, d), lambda i: (0, 0))],
        out_specs=pl.BlockSpec((bm, d), lambda i: (i, 0)),
        out_shape=jax.ShapeDtypeStruct((t, d), F32),
        compiler_params=_params("parallel"),
        name="final_norm",
    )(x2, gain)


def kernel(x, c, ada_w, ada_b, mix_norm, ffn_norm, par_w_in, par_w_out, hg_lb_logits, hg_out_norm, sg_w_in, sg_v_gain, sg_v_bias, sg_w_pos, sg_b_pos, sg_w_out, ffn_up, ffn_conv_w, ffn_conv_b, ffn_down, final_norm):
    b, s, d = x.shape
    depth = ada_w.shape[0]
    t = b * s
    sb_w = d // 2
    bm = min(ROW_TILE, s)
    bm_sg = min(SG_ROW_TILE, s)

    c_pad = jnp.pad(c, ((0, ADA_PAD_ROWS - b), (0, 0)))
    mod = _ada_mod(c_pad, ada_w, ada_b)
    mod4 = mod.reshape(depth, ADA_PAD_ROWS, 1, 6 * d)

    ffn_up_bf16 = ffn_up.astype(BF16)
    ffn_down_bf16 = ffn_down.astype(BF16)
    x2 = x.reshape(t, d)
    for layer in range(depth):
        j = layer // 2
        gain1 = mix_norm[layer].reshape(1, d)
        if layer % 2 == 0:
            qkv, hg = _par_proj(x2, gain1, mod4, layer, par_w_in[j].astype(BF16), 3 * sb_w,
                                s, bm, 1024)
            o_sb = _sb_attention(qkv.reshape(b, s, 3 * sb_w), SB_HEADS, min(SB_QUERY_TILE, s),
                                 SB_HEADS_PER_STEP)
            o_hg = _hgrn2(hg.reshape(b, s, hg.shape[1]), hg_lb_logits,
                          hg_out_norm[j].reshape(1, -1), j, HG_HEADS, min(HG_TIME_TILE, s))
            x2 = _residual_matmul([o_sb.reshape(t, sb_w), o_hg.reshape(t, -1)],
                                  par_w_out.astype(BF16), j, x2, mod4, layer, 2, s, bm, 1024,
                                  "par_out")
        else:
            u, v = _sg_in(x2, gain1, mod4, layer, sg_w_in[j].astype(BF16),
                          sg_v_gain[j].reshape(1, -1), sg_v_bias[j].reshape(1, -1), s, bm_sg)
            x2 = _sg_out(u, v, sg_w_pos[j], sg_b_pos[j].T, sg_w_out[j].astype(BF16), x2, mod4,
                         layer, s, bm_sg, 1024)
        act = _ffn_up(x2, ffn_norm[layer].reshape(1, d), mod4, layer, ffn_up_bf16,
                      ffn_conv_w, ffn_conv_b, s, bm, 512)
        x2 = _residual_matmul([act], ffn_down_bf16, layer, x2, mod4, layer, 5, s, bm, 512,
                              "ffn_down")
    out = _final_norm(x2, final_norm.reshape(1, d), bm_sg)
    return out.reshape(b, s, d)
```

```python
import functools

import jax
import jax.numpy as jnp
from jax import lax
from jax.experimental import pallas as pl
from jax.experimental.pallas import tpu as pltpu

F32 = jnp.float32
BF16 = jnp.bfloat16

NORM_EPS = 1e-6
SB_HEADS = 8
HG_HEADS = 8
HG_CHUNK = 64
SG_GROUPS = 8
SG_CHUNK = 128
CONV_WIDTH = 3
HALO_ROWS = 8

LOG2E = 1.4426950408889634
NORM_ROW_BLOCK = 64
SB_HEADS_PER_STEP = 8
ROW_TILE = 1024
SG_ROW_TILE = 512
SB_QUERY_TILE = 256
HG_TIME_TILE = 512

V7X_VMEM_LIMIT_BYTES =56 * 1024 * 1024
ADA_PAD_ROWS = 16


def _params(*sem):
    return pltpu.CompilerParams(dimension_semantics=sem, vmem_limit_bytes=V7X_VMEM_LIMIT_BYTES)


def _silu(x):
    return x * jax.nn.sigmoid(x)


def _dot(a, b):
    return jnp.dot(a, b, preferred_element_type=F32)


def _dot_nt(a, b):
    return lax.dot_general(a, b, (((1,), (1,)), ((), ())), preferred_element_type=F32)


def _split_dot(a_f32, b_bf16):
    hi = a_f32.astype(BF16)
    lo = (a_f32 - hi.astype(F32)).astype(BF16)
    return _dot(hi, b_bf16) + _dot(lo, b_bf16)


def _split_dot_left(b_bf16, a_f32):
    hi = a_f32.astype(BF16)
    lo = (a_f32 - hi.astype(F32)).astype(BF16)
    return _dot(b_bf16, hi) + _dot(b_bf16, lo)


def _modulated_rms_norm(x, gain_scale, shift):
    inv = lax.rsqrt(jnp.mean(x * x, axis=-1, keepdims=True) + NORM_EPS)
    return x * inv * gain_scale + shift


def _norm_rows(x_ref, h_ref, gain_ref, sc_ref, sh_ref):
    rows = x_ref.shape[0]
    rb = min(NORM_ROW_BLOCK, rows)
    gain_scale = gain_ref[...] * (1.0 + sc_ref[...])
    shift = sh_ref[...]

    def body(r, _):
        rs = pl.ds(pl.multiple_of(r * rb, rb), rb)
        h_ref[rs, :] = _modulated_rms_norm(x_ref[rs, :], gain_scale, shift).astype(h_ref.dtype)
        return 0

    lax.fori_loop(0, rows // rb, body, 0, unroll=min(2, rows // rb))


def _ada_kernel(c_ref, w_ref, b_ref, o_ref):
    cond = _silu(c_ref[...]).astype(BF16)
    o_ref[...] = _dot(cond, w_ref[...].astype(BF16)) + b_ref[...]


def _ada_mod(c_pad, ada_w, ada_b):
    depth, d, n = ada_w.shape
    tn = 1024
    return pl.pallas_call(
        _ada_kernel,
        grid=(depth, n // tn),
        in_specs=[
            pl.BlockSpec((ADA_PAD_ROWS, d), lambda l, j: (0, 0)),
            pl.BlockSpec((None, d, tn), lambda l, j: (l, 0, j)),
            pl.BlockSpec((None, 1, tn), lambda l, j: (l, 0, j)),
        ],
        out_specs=pl.BlockSpec((None, ADA_PAD_ROWS, tn), lambda l, j: (l, 0, j)),
        out_shape=jax.ShapeDtypeStruct((depth, ADA_PAD_ROWS, n), F32),
        compiler_params=_params("parallel", "arbitrary"),
        name="ada_mod",
    )(c_pad, ada_w, ada_b.reshape(depth, 1, n))


def _mod_spec(layer, slot, d, bn, tiles_per_seq, with_j):
    nb = d // bn
    if with_j:
        return pl.BlockSpec((None, None, 1, bn),
                            lambda i, j: (layer, i // tiles_per_seq, 0, slot * nb + j))
    return pl.BlockSpec((None, None, 1, bn),
                        lambda i, j: (layer, i // tiles_per_seq, 0, slot * nb))


def _par_proj_kernel(x_ref, gain_ref, sc_ref, sh_ref, w_ref, qkv_ref, hg_ref, h_ref, *, n_sb):
    j = pl.program_id(1)

    @pl.when(j == 0)
    def _():
        _norm_rows(x_ref, h_ref, gain_ref, sc_ref, sh_ref)

    @pl.when(j < n_sb)
    def _():
        qkv_ref[...] = _dot(h_ref[...], w_ref[...]).astype(qkv_ref.dtype)

    @pl.when(j >= n_sb)
    def _():
        hg_ref[...] = _dot(h_ref[...], w_ref[...])


def _par_proj(x2, gain, mod4, layer, w, sb_cols, seq, bm, bn):
    t, d = x2.shape
    n = w.shape[1]
    tps = seq // bm
    n_sb = sb_cols // bn
    return pl.pallas_call(
        functools.partial(_par_proj_kernel, n_sb=n_sb),
        grid=(t // bm, n // bn),
        in_specs=[
            pl.BlockSpec((bm, d), lambda i, j: (i, 0)),
            pl.BlockSpec((1, d), lambda i, j: (0, 0)),
            _mod_spec(layer, 1, d, d, tps, False),
            _mod_spec(layer, 0, d, d, tps, False),
            pl.BlockSpec((d, bn), lambda i, j: (0, j)),
        ],
        out_specs=[
            pl.BlockSpec((bm, bn), lambda i, j: (i, jnp.minimum(j, n_sb - 1))),
            pl.BlockSpec((bm, bn), lambda i, j: (i, jnp.maximum(j - n_sb, 0))),
        ],
        out_shape=[jax.ShapeDtypeStruct((t, sb_cols), BF16),
                   jax.ShapeDtypeStruct((t, n - sb_cols), F32)],
        scratch_shapes=[pltpu.VMEM((bm, d), BF16)],
        compiler_params=_params("parallel", "arbitrary"),
        name="par_proj",
    )(x2, gain, mod4, mod4, w)


def _sb_kernel(q_ref, k_ref, v_ref, o_ref, acc_ref, *, tq, hp, dh, neg_scale_log2e):
    qi = pl.program_id(2)
    row = lax.broadcasted_iota(jnp.int32, (tq, tq), 0)
    col = lax.broadcasted_iota(jnp.int32, (tq, tq), 1)
    strict = col < row
    after = (row > col).astype(BF16)

    def block(kb, carries, masked):
        start = pl.multiple_of(kb * tq, tq)
        st = [{} for _ in range(hp)]
        new = [None] * hp

        def stage_a(h, cs):
            st[h]["y"] = _dot_nt(q_ref[:, cs], k_ref[pl.ds(start, tq), cs]) * neg_scale_log2e

        def stage_b(h, cs):
            y = st[h].pop("y")
            m = jnp.minimum(y, 0.0)
            d = m - y
            t = jnp.log(1.0 + jnp.exp2(m + d)) * LOG2E
            log_keep = m - t
            if masked:
                log_keep = jnp.where(strict, log_keep, 0.0)
            st[h].update(keep=log_keep.astype(BF16), log_beta=d - t,
                         first=log_keep[:, :1].astype(BF16).astype(F32))

        def stage_c(h, cs):
            st[h]["tail"] = _dot(st[h].pop("keep"), after)

        def stage_d(h, cs):
            tail = st[h].pop("tail")
            w = jnp.exp2(st[h].pop("log_beta") + tail + carries[h])
            if masked:
                w = jnp.where(strict, w, 0.0)
            st[h]["w"] = w.astype(BF16)
            new[h] = carries[h] + tail[:, :1] + st[h].pop("first")

        def stage_e(h, cs):
            pv = _dot(st[h].pop("w"), v_ref[pl.ds(start, tq), cs])
            if masked:
                acc_ref[h] = pv
            else:
                acc_ref[h] += pv

        stages = (stage_a, stage_b, stage_c, stage_d, stage_e)
        for slot in range(hp + len(stages) - 1):
            for depth, stage in enumerate(stages):
                h = slot - depth
                if 0 <= h < hp:
                    stage(h, slice(h * dh, (h + 1) * dh))
        return tuple(new)

    carries = block(qi, (jnp.zeros((tq, 1), F32),) * hp, True)
    lax.fori_loop(0, qi, lambda step, cr: block(qi - 1 - step, cr, False), carries)
    for h in range(hp):
        o_ref[:, h * dh:(h + 1) * dh] = acc_ref[h].astype(o_ref.dtype)


def _sb_attention(qkv, heads, tq, hp):
    b, s, w3 = qkv.shape
    dh = w3 // (3 * heads)
    ng = heads // hp
    kern = functools.partial(_sb_kernel, tq=tq, hp=hp, dh=dh,
                             neg_scale_log2e=-(dh ** -0.5) * LOG2E)
    return pl.pallas_call(
        kern,
        grid=(b, ng, s // tq),
        in_specs=[
            pl.BlockSpec((None, tq, hp * dh), lambda bi, g, qi: (bi, qi, g)),
            pl.BlockSpec((None, s, hp * dh), lambda bi, g, qi: (bi, 0, ng + g)),
            pl.BlockSpec((None, s, hp * dh), lambda bi, g, qi: (bi, 0, 2 * ng + g)),
        ],
        out_specs=pl.BlockSpec((None, tq, hp * dh), lambda bi, g, qi: (bi, qi, g)),
        out_shape=jax.ShapeDtypeStruct((b, s, heads * dh), BF16),
        scratch_shapes=[pltpu.VMEM((hp, tq, dh), F32)],
        compiler_params=_params("parallel", "parallel", "arbitrary"),
        name="sb_attention",
    )(qkv, qkv, qkv)


def _hgrn_kernel(lbl_ref, gain_ref, q_ref, f_ref, i_ref, g_ref, o_ref, st_ref, *,
                 layer_j, heads, chunk, tt):
    @pl.when(pl.program_id(1) == 0)
    def _():
        st_ref[...] = jnp.zeros_like(st_ref)

    logits = lbl_ref[...]
    e = jnp.exp(logits - jnp.max(logits, axis=0, keepdims=True))
    sm = e / jnp.sum(e, axis=0, keepdims=True)
    lb = jnp.sum(sm[:layer_j + 1], axis=0, keepdims=True)
    gain = gain_ref[...]

    dk = q_ref.shape[1] // heads
    dv = i_ref.shape[1] // heads
    trow = lax.broadcasted_iota(jnp.int32, (chunk, chunk), 0)
    tcol = lax.broadcasted_iota(jnp.int32, (chunk, chunk), 1)
    causal = tcol <= trow
    lower = causal.astype(BF16)

    def chunk_body(c, _):
        r0 = pl.multiple_of(c * chunk, chunk)
        rows = pl.ds(r0, chunk)
        f = lb + (1.0 - lb) * jax.nn.sigmoid(f_ref[rows, :])
        log_f = jnp.log(f)
        k = 1.0 - f
        qf = _silu(q_ref[rows, :])
        g_cum = _split_dot_left(lower, log_f)
        g_last = g_cum[chunk - 1:chunk, :]
        q_dec = (qf * jnp.exp(g_cum)).astype(BF16)
        k_inv = (k * jnp.exp(-g_cum)).astype(BF16)
        k_end = (k * jnp.exp(g_last - g_cum)).astype(BF16)
        decay = jnp.exp(g_last)
        v_all = i_ref[rows, :]
        gate = _silu(g_ref[rows, :])
        for h in range(heads):
            ks = slice(h * dk, (h + 1) * dk)
            vs = slice(h * dv, (h + 1) * dv)
            v = v_all[:, vs].astype(BF16)
            v_t = v_all[:, vs].T.astype(BF16)
            scores = jnp.where(causal, _dot_nt(q_dec[:, ks], k_inv[:, ks]), 0.0)
            st = st_ref[h]
            o = _dot(scores.astype(BF16), v) + _dot_nt(q_dec[:, ks], st.astype(BF16))
            st_ref[h] = decay[:, ks] * st + _dot(v_t, k_end[:, ks])
            o = o * lax.rsqrt(jnp.mean(o * o, axis=-1, keepdims=True) + NORM_EPS) * gain[:, vs]
            o_ref[rows, vs] = (o * gate[:, vs]).astype(o_ref.dtype)
        return 0

    lax.fori_loop(0, tt // chunk, chunk_body, 0)


def _hgrn2(hg, lb_logits, out_gain, layer_j, heads, tt):
    b, s, w4 = hg.shape
    kw = lb_logits.shape[1]
    vw = out_gain.shape[1]
    assert w4 == 2 * kw + 2 * vw and kw == vw
    nb = w4 // kw
    dk = kw // heads
    dv = vw // heads
    kern = functools.partial(_hgrn_kernel, layer_j=layer_j, heads=heads, chunk=HG_CHUNK, tt=tt)
    col = lambda k: pl.BlockSpec((None, tt, kw), lambda bi, ti: (bi, ti, k))
    assert nb == 4
    return pl.pallas_call(
        kern,
        grid=(b, s // tt),
        in_specs=[
            pl.BlockSpec(lb_logits.shape, lambda bi, ti: (0, 0)),
            pl.BlockSpec((1, vw), lambda bi, ti: (0, 0)),
            col(0), col(1), col(2), col(3),
        ],
        out_specs=pl.BlockSpec((None, tt, vw), lambda bi, ti: (bi, ti, 0)),
        out_shape=jax.ShapeDtypeStruct((b, s, vw), BF16),
        scratch_shapes=[pltpu.VMEM((heads, dv, dk), F32)],
        compiler_params=_params("parallel", "arbitrary"),
        name="hgrn2",
    )(lb_logits, out_gain, hg, hg, hg, hg)


def _residual_matmul_kernel(*refs, n_lhs):
    a_refs = refs[:n_lhs]
    w_refs = refs[n_lhs:2 * n_lhs]
    x_ref, g_ref, o_ref = refs[2 * n_lhs:]
    acc = _dot(a_refs[0][...], w_refs[0][...])
    for a_ref, w_ref in zip(a_refs[1:], w_refs[1:]):
        acc = acc + _dot(a_ref[...], w_ref[...])
    o_ref[...] = x_ref[...] + g_ref[...] * acc


def _residual_matmul(lhs_list, w, w_index, x2, mod4, layer, slot, seq, bm, bn, name):
    t, d = x2.shape
    n_lhs = len(lhs_list)
    tps = seq // bm
    in_specs = []
    row0 = 0
    w_specs = []
    for a in lhs_list:
        ka = a.shape[1]
        in_specs.append(pl.BlockSpec((bm, ka), lambda i, j: (i, 0)))
        assert row0 % ka == 0
        w_specs.append(pl.BlockSpec((None, ka, bn), functools.partial(
            lambda i, j, r: (w_index, r, j), r=row0 // ka)))
        row0 += ka
    assert row0 == w.shape[1]
    in_specs += w_specs
    in_specs += [
        pl.BlockSpec((bm, bn), lambda i, j: (i, j)),
        _mod_spec(layer, slot, d, bn, tps, True),
    ]
    return pl.pallas_call(
        functools.partial(_residual_matmul_kernel, n_lhs=n_lhs),
        grid=(t // bm, d // bn),
        in_specs=in_specs,
        out_specs=pl.BlockSpec((bm, bn), lambda i, j: (i, j)),
        out_shape=jax.ShapeDtypeStruct((t, d), F32),
        compiler_params=_params("parallel", "arbitrary"),
        name=name,
    )(*lhs_list, *([w] * n_lhs), x2, mod4)


def _ffn_up_kernel(x_ref, xh_ref, gain_ref, sc_ref, sh_ref, wg_ref, wv_ref, cwg_ref, cwv_ref,
                   cbg_ref, cbv_ref, o_ref, h_ref, hh_ref, eg_ref, ev_ref, *, bm, tiles_per_seq):
    i = pl.program_id(0)

    @pl.when(pl.program_id(1) == 0)
    def _():
        _norm_rows(x_ref, h_ref, gain_ref, sc_ref, sh_ref)
        _norm_rows(xh_ref, hh_ref, gain_ref, sc_ref, sh_ref)

    first = (i % tiles_per_seq) == 0
    h = h_ref[...]
    hh = hh_ref[...]
    for w_ref, e_ref in ((wg_ref, eg_ref), (wv_ref, ev_ref)):
        w = w_ref[...]
        e_ref[pl.ds(0, HALO_ROWS), :] = jnp.where(first, 0.0, _dot(hh, w))
        e_ref[pl.ds(HALO_ROWS, bm), :] = _dot(h, w)

    def conv(e_ref, cw_ref, cb_ref):
        out = cb_ref[...]
        for tap in range(CONV_WIDTH):
            off = HALO_ROWS - (CONV_WIDTH - 1) + tap
            out = out + cw_ref[tap:tap + 1, :] * e_ref[pl.ds(off, bm), :]
        return out

    gate = conv(eg_ref, cwg_ref, cbg_ref)
    val = conv(ev_ref, cwv_ref, cbv_ref)
    o_ref[...] = (_silu(gate) * val).astype(o_ref.dtype)


def _ffn_up(x2, gain, mod4, layer, w_up, conv_w, conv_b, seq, bm, bn):
    t, d = x2.shape
    f = w_up.shape[2] // 2
    nj = f // bn
    tps = seq // bm
    hb = bm // HALO_ROWS
    depth = conv_w.shape[0]
    conv_b3 = conv_b.reshape(depth, 1, 2 * f)
    kern = functools.partial(_ffn_up_kernel, bm=bm, tiles_per_seq=tps)
    e_buf = pltpu.VMEM((bm + HALO_ROWS, bn), F32)
    return pl.pallas_call(
        kern,
        grid=(t // bm, nj),
        in_specs=[
            pl.BlockSpec((bm, d), lambda i, j: (i, 0)),
            pl.BlockSpec((HALO_ROWS, d), lambda i, j: (jnp.maximum(i * hb - 1, 0), 0)),
            pl.BlockSpec((1, d), lambda i, j: (0, 0)),
            _mod_spec(layer, 4, d, d, tps, False),
            _mod_spec(layer, 3, d, d, tps, False),
            pl.BlockSpec((None, d, bn), lambda i, j: (layer, 0, j)),
            pl.BlockSpec((None, d, bn), lambda i, j: (layer, 0, nj + j)),
            pl.BlockSpec((None, CONV_WIDTH, bn), lambda i, j: (layer, 0, j)),
            pl.BlockSpec((None, CONV_WIDTH, bn), lambda i, j: (layer, 0, nj + j)),
            pl.BlockSpec((None, 1, bn), lambda i, j: (layer, 0, j)),
            pl.BlockSpec((None, 1, bn), lambda i, j: (layer, 0, nj + j)),
        ],
        out_specs=pl.BlockSpec((bm, bn), lambda i, j: (i, j)),
        out_shape=jax.ShapeDtypeStruct((t, f), BF16),
        scratch_shapes=[
            pltpu.VMEM((bm, d), BF16),
            pltpu.VMEM((HALO_ROWS, d), BF16),
            e_buf, e_buf,
        ],
        compiler_params=_params("parallel", "arbitrary"),
        name="ffn_up",
    )(x2, x2, gain, mod4, mod4, w_up, w_up, conv_w, conv_w, conv_b3, conv_b3)


def _gelu(x):
    return 0.5 * x * (1.0 + lax.erf(x * (2.0 ** -0.5)))


def _sg_in_kernel(x_ref, gain_ref, sc_ref, sh_ref, w_ref, vg_ref, vb_ref, u_ref, v_ref, h_ref):
    j = pl.program_id(1)

    @pl.when(j == 0)
    def _():
        _norm_rows(x_ref, h_ref, gain_ref, sc_ref, sh_ref)

    z = _gelu(_dot(h_ref[...], w_ref[...]))

    @pl.when(j == 0)
    def _():
        u_ref[...] = z.astype(u_ref.dtype)

    @pl.when(j == 1)
    def _():
        zc = z - jnp.mean(z, axis=-1, keepdims=True)
        y = zc * lax.rsqrt(jnp.mean(zc * zc, axis=-1, keepdims=True) + NORM_EPS)
        v_ref[...] = (y * vg_ref[...] + vb_ref[...]).astype(v_ref.dtype)


def _sg_in(x2, gain, mod4, layer, w_in, v_gain, v_bias, seq, bm):
    t, d = x2.shape
    width = w_in.shape[1] // 2
    tps = seq // bm
    out = jax.ShapeDtypeStruct((t, width), BF16)
    return pl.pallas_call(
        _sg_in_kernel,
        grid=(t // bm, 2),
        in_specs=[
            pl.BlockSpec((bm, d), lambda i, j: (i, 0)),
            pl.BlockSpec((1, d), lambda i, j: (0, 0)),
            _mod_spec(layer, 1, d, d, tps, False),
            _mod_spec(layer, 0, d, d, tps, False),
            pl.BlockSpec((d, width), lambda i, j: (0, j)),
            pl.BlockSpec((1, width), lambda i, j: (0, 0)),
            pl.BlockSpec((1, width), lambda i, j: (0, 0)),
        ],
        out_specs=[pl.BlockSpec((bm, width), lambda i, j: (i, 0)),
                   pl.BlockSpec((bm, width), lambda i, j: (i, 0))],
        out_shape=[out, out],
        scratch_shapes=[pltpu.VMEM((bm, d), BF16)],
        compiler_params=_params("parallel", "arbitrary"),
        name="sg_in",
    )(x2, gain, mod4, mod4, w_in, v_gain, v_bias)


def _sg_out_kernel(u_ref, v_ref, wp_ref, bp_ref, w_ref, x_ref, g_ref, o_ref, m_ref, *,
                   bm, groups, chunk):
    @pl.when(pl.program_id(1) == 0)
    def _():
        gw = u_ref.shape[1] // groups
        row = lax.broadcasted_iota(jnp.int32, (chunk, chunk), 0)
        col = lax.broadcasted_iota(jnp.int32, (chunk, chunk), 1)
        causal = col <= row
        for g in range(groups):
            w_pos = jnp.where(causal, wp_ref[g], 0.0).astype(BF16)
            bias = bp_ref[:, g:g + 1]
            cs = slice(g * gw, (g + 1) * gw)
            for r in range(bm // chunk):
                rs = slice(r * chunk, (r + 1) * chunk)
                mixed = _dot(w_pos, v_ref[rs, cs]) + bias
                m_ref[rs, cs] = (u_ref[rs, cs].astype(F32) * mixed).astype(BF16)

    o_ref[...] = x_ref[...] + g_ref[...] * _dot(m_ref[...], w_ref[...])


def _sg_out(u, v, w_pos, b_pos_t, w_out, x2, mod4, layer, seq, bm, bn):
    t, d = x2.shape
    width = u.shape[1]
    tps = seq // bm
    kern = functools.partial(_sg_out_kernel, bm=bm, groups=SG_GROUPS, chunk=SG_CHUNK)
    return pl.pallas_call(
        kern,
        grid=(t // bm, d // bn),
        in_specs=[
            pl.BlockSpec((bm, width), lambda i, j: (i, 0)),
            pl.BlockSpec((bm, width), lambda i, j: (i, 0)),
            pl.BlockSpec(w_pos.shape, lambda i, j: (0, 0, 0)),
            pl.BlockSpec(b_pos_t.shape, lambda i, j: (0, 0)),
            pl.BlockSpec((width, bn), lambda i, j: (0, j)),
            pl.BlockSpec((bm, bn), lambda i, j: (i, j)),
            _mod_spec(layer, 2, d, bn, tps, True),
        ],
        out_specs=pl.BlockSpec((bm, bn), lambda i, j: (i, j)),
        out_shape=jax.ShapeDtypeStruct((t, d), F32),
        scratch_shapes=[pltpu.VMEM((bm, width), BF16)],
        compiler_params=_params("parallel", "arbitrary"),
        name="sg_out",
    )(u, v, w_pos, b_pos_t, w_out, x2, mod4)


def _final_norm_kernel(x_ref, gain_ref, o_ref):
    x = x_ref[...]
    o_ref[...] = x * lax.rsqrt(jnp.mean(x * x, axis=-1, keepdims=True) + NORM_EPS) * gain_ref[...]


def _final_norm(x2, gain, bm):
    t, d = x2.shape
    return pl.pallas_call(
        _final_norm_kernel,
        grid=(t // bm,),
        in_specs=[pl.BlockSpec((bm, d), lambda i: (i, 0)),
                  pl.BlockSpec((1, d), lambda i: (0, 0))],
        out_specs=pl.BlockSpec((bm, d), lambda i: (i, 0)),
        out_shape=jax.ShapeDtypeStruct((t, d), F32),
        compiler_params=_params("parallel"),
        name="final_norm",
    )(x2, gain)


def kernel(x, c, ada_w, ada_b, mix_norm, ffn_norm, par_w_in, par_w_out, hg_lb_logits, hg_out_norm, sg_w_in, sg_v_gain, sg_v_bias, sg_w_pos, sg_b_pos, sg_w_out, ffn_up, ffn_conv_w, ffn_conv_b, ffn_down, final_norm):
    b, s, d = x.shape
    depth = ada_w.shape[0]
    t = b * s
    sb_w = d // 2
    bm = min(ROW_TILE, s)
    bm_sg = min(SG_ROW_TILE, s)

    c_pad = jnp.pad(c, ((0, ADA_PAD_ROWS - b), (0, 0)))
    mod = _ada_mod(c_pad, ada_w, ada_b)
    mod4 = mod.reshape(depth, ADA_PAD_ROWS, 1, 6 * d)

    ffn_up_bf16 = ffn_up.astype(BF16)
    ffn_down_bf16 = ffn_down.astype(BF16)
    x2 = x.reshape(t, d)
    for layer in range(depth):
        j = layer // 2
        gain1 = mix_norm[layer].reshape(1, d)
        if layer % 2 == 0:
            qkv, hg = _par_proj(x2, gain1, mod4, layer, par_w_in[j].astype(BF16), 3 * sb_w,
                                s, bm, 1024)
            o_sb = _sb_attention(qkv.reshape(b, s, 3 * sb_w), SB_HEADS, min(SB_QUERY_TILE, s),
                                 SB_HEADS_PER_STEP)
            o_hg = _hgrn2(hg.reshape(b, s, hg.shape[1]), hg_lb_logits,
                          hg_out_norm[j].reshape(1, -1), j, HG_HEADS, min(HG_TIME_TILE, s))
            x2 = _residual_matmul([o_sb.reshape(t, sb_w), o_hg.reshape(t, -1)],
                                  par_w_out.astype(BF16), j, x2, mod4, layer, 2, s, bm, 1024,
                                  "par_out")
        else:
            u, v = _sg_in(x2, gain1, mod4, layer, sg_w_in[j].astype(BF16),
                          sg_v_gain[j].reshape(1, -1), sg_v_bias[j].reshape(1, -1), s, bm_sg)
            x2 = _sg_out(u, v, sg_w_pos[j], sg_b_pos[j].T, sg_w_out[j].astype(BF16), x2, mod4,
                         layer, s, bm_sg, 1024)
        act = _ffn_up(x2, ffn_norm[layer].reshape(1, d), mod4, layer, ffn_up_bf16,
                      ffn_conv_w, ffn_conv_b, s, bm, 512)
        x2 = _residual_matmul([act], ffn_down_bf16, layer, x2, mod4, layer, 5, s, bm, 512,
                              "ffn_down")
    out = _final_norm(x2, final_norm.reshape(1, d), bm_sg)
    return out.reshape(b, s, d)
```

```python
import functools

import jax
import jax.numpy as jnp
from jax import lax
from jax.experimental import pallas as pl
from jax.experimental.pallas import tpu as pltpu

F32 = jnp.float32
BF16 = jnp.bfloat16

NORM_EPS = 1e-6
SB_HEADS = 8
HG_HEADS = 8
HG_CHUNK = 64
SG_GROUPS = 8
SG_CHUNK = 128
CONV_WIDTH = 3
HALO_ROWS = 8

LOG2E = 1.4426950408889634
NORM_ROW_BLOCK = 64
SB_HEADS_PER_STEP = 8
ROW_TILE = 1024
SG_ROW_TILE = 512
SB_QUERY_TILE = 256
HG_TIME_TILE = 512

V7X_VMEM_LIMIT_BYTES =56 * 1024 * 1024
ADA_PAD_ROWS = 16


def _params(*sem):
    return pltpu.CompilerParams(dimension_semantics=sem, vmem_limit_bytes=V7X_VMEM_LIMIT_BYTES)


def _silu(x):
    return x * jax.nn.sigmoid(x)


def _dot(a, b):
    return jnp.dot(a, b, preferred_element_type=F32)


def _dot_nt(a, b):
    return lax.dot_general(a, b, (((1,), (1,)), ((), ())), preferred_element_type=F32)


def _split_dot(a_f32, b_bf16):
    hi = a_f32.astype(BF16)
    lo = (a_f32 - hi.astype(F32)).astype(BF16)
    return _dot(hi, b_bf16) + _dot(lo, b_bf16)


def _split_dot_left(b_bf16, a_f32):
    hi = a_f32.astype(BF16)
    lo = (a_f32 - hi.astype(F32)).astype(BF16)
    return _dot(b_bf16, hi) + _dot(b_bf16, lo)


def _modulated_rms_norm(x, gain_scale, shift):
    inv = lax.rsqrt(jnp.mean(x * x, axis=-1, keepdims=True) + NORM_EPS)
    return x * inv * gain_scale + shift


def _norm_rows(x_ref, h_ref, gain_ref, sc_ref, sh_ref):
    rows = x_ref.shape[0]
    rb = min(NORM_ROW_BLOCK, rows)
    gain_scale = gain_ref[...] * (1.0 + sc_ref[...])
    shift = sh_ref[...]

    def body(r, _):
        rs = pl.ds(pl.multiple_of(r * rb, rb), rb)
        h_ref[rs, :] = _modulated_rms_norm(x_ref[rs, :], gain_scale, shift).astype(h_ref.dtype)
        return 0

    lax.fori_loop(0, rows // rb, body, 0, unroll=min(2, rows // rb))


def _ada_kernel(c_ref, w_ref, b_ref, o_ref):
    cond = _silu(c_ref[...]).astype(BF16)
    o_ref[...] = _dot(cond, w_ref[...].astype(BF16)) + b_ref[...]


def _ada_mod(c_pad, ada_w, ada_b):
    depth, d, n = ada_w.shape
    tn = 1024
    return pl.pallas_call(
        _ada_kernel,
        grid=(depth, n // tn),
        in_specs=[
            pl.BlockSpec((ADA_PAD_ROWS, d), lambda l, j: (0, 0)),
            pl.BlockSpec((None, d, tn), lambda l, j: (l, 0, j)),
            pl.BlockSpec((None, 1, tn), lambda l, j: (l, 0, j)),
        ],
        out_specs=pl.BlockSpec((None, ADA_PAD_ROWS, tn), lambda l, j: (l, 0, j)),
        out_shape=jax.ShapeDtypeStruct((depth, ADA_PAD_ROWS, n), F32),
        compiler_params=_params("parallel", "arbitrary"),
        name="ada_mod",
    )(c_pad, ada_w, ada_b.reshape(depth, 1, n))


def _mod_spec(layer, slot, d, bn, tiles_per_seq, with_j):
    nb = d // bn
    if with_j:
        return pl.BlockSpec((None, None, 1, bn),
                            lambda i, j: (layer, i // tiles_per_seq, 0, slot * nb + j))
    return pl.BlockSpec((None, None, 1, bn),
                        lambda i, j: (layer, i // tiles_per_seq, 0, slot * nb))


def _par_proj_kernel(x_ref, gain_ref, sc_ref, sh_ref, w_ref, qkv_ref, hg_ref, h_ref, *, n_sb):
    j = pl.program_id(1)

    @pl.when(j == 0)
    def _():
        _norm_rows(x_ref, h_ref, gain_ref, sc_ref, sh_ref)

    @pl.when(j < n_sb)
    def _():
        qkv_ref[...] = _dot(h_ref[...], w_ref[...]).astype(qkv_ref.dtype)

    @pl.when(j >= n_sb)
    def _():
        hg_ref[...] = _dot(h_ref[...], w_ref[...])


def _par_proj(x2, gain, mod4, layer, w, sb_cols, seq, bm, bn):
    t, d = x2.shape
    n = w.shape[1]
    tps = seq // bm
    n_sb = sb_cols // bn
    return pl.pallas_call(
        functools.partial(_par_proj_kernel, n_sb=n_sb),
        grid=(t // bm, n // bn),
        in_specs=[
            pl.BlockSpec((bm, d), lambda i, j: (i, 0)),
            pl.BlockSpec((1, d), lambda i, j: (0, 0)),
            _mod_spec(layer, 1, d, d, tps, False),
            _mod_spec(layer, 0, d, d, tps, False),
            pl.BlockSpec((d, bn), lambda i, j: (0, j)),
        ],
        out_specs=[
            pl.BlockSpec((bm, bn), lambda i, j: (i, jnp.minimum(j, n_sb - 1))),
            pl.BlockSpec((bm, bn), lambda i, j: (i, jnp.maximum(j - n_sb, 0))),
        ],
        out_shape=[jax.ShapeDtypeStruct((t, sb_cols), BF16),
                   jax.ShapeDtypeStruct((t, n - sb_cols), F32)],
        scratch_shapes=[pltpu.VMEM((bm, d), BF16)],
        compiler_params=_params("parallel", "arbitrary"),
        name="par_proj",
    )(x2, gain, mod4, mod4, w)


def _sb_kernel(q_ref, k_ref, v_ref, o_ref, acc_ref, *, tq, hp, dh, neg_scale_log2e):
    qi = pl.program_id(2)
    row = lax.broadcasted_iota(jnp.int32, (tq, tq), 0)
    col = lax.broadcasted_iota(jnp.int32, (tq, tq), 1)
    strict = col < row
    after = (row > col).astype(BF16)

    def block(kb, carries, masked):
        start = pl.multiple_of(kb * tq, tq)
        st = [{} for _ in range(hp)]
        new = [None] * hp

        def stage_a(h, cs):
            st[h]["y"] = _dot_nt(q_ref[:, cs], k_ref[pl.ds(start, tq), cs]) * neg_scale_log2e

        def stage_b(h, cs):
            y = st[h].pop("y")
            m = jnp.minimum(y, 0.0)
            d = m - y
            t = jnp.log(1.0 + jnp.exp2(m + d)) * LOG2E
            log_keep = m - t
            if masked:
                log_keep = jnp.where(strict, log_keep, 0.0)
            st[h].update(keep=log_keep.astype(BF16), log_beta=d - t,
                         first=log_keep[:, :1].astype(BF16).astype(F32))

        def stage_c(h, cs):
            st[h]["tail"] = _dot(st[h].pop("keep"), after)

        def stage_d(h, cs):
            tail = st[h].pop("tail")
            w = jnp.exp2(st[h].pop("log_beta") + tail + carries[h])
            if masked:
                w = jnp.where(strict, w, 0.0)
            st[h]["w"] = w.astype(BF16)
            new[h] = carries[h] + tail[:, :1] + st[h].pop("first")

        def stage_e(h, cs):
            pv = _dot(st[h].pop("w"), v_ref[pl.ds(start, tq), cs])
            if masked:
                acc_ref[h] = pv
            else:
                acc_ref[h] += pv

        stages = (stage_a, stage_b, stage_c, stage_d, stage_e)
        for slot in range(hp + len(stages) - 1):
            for depth, stage in enumerate(stages):
                h = slot - depth
                if 0 <= h < hp:
                    stage(h, slice(h * dh, (h + 1) * dh))
        return tuple(new)

    carries = block(qi, (jnp.zeros((tq, 1), F32),) * hp, True)
    lax.fori_loop(0, qi, lambda step, cr: block(qi - 1 - step, cr, False), carries)
    for h in range(hp):
        o_ref[:, h * dh:(h + 1) * dh] = acc_ref[h].astype(o_ref.dtype)


def _sb_attention(qkv, heads, tq, hp):
    b, s, w3 = qkv.shape
    dh = w3 // (3 * heads)
    ng = heads // hp
    kern = functools.partial(_sb_kernel, tq=tq, hp=hp, dh=dh,
                             neg_scale_log2e=-(dh ** -0.5) * LOG2E)
    return pl.pallas_call(
        kern,
        grid=(b, ng, s // tq),
        in_specs=[
            pl.BlockSpec((None, tq, hp * dh), lambda bi, g, qi: (bi, qi, g)),
            pl.BlockSpec((None, s, hp * dh), lambda bi, g, qi: (bi, 0, ng + g)),
            pl.BlockSpec((None, s, hp * dh), lambda bi, g, qi: (bi, 0, 2 * ng + g)),
        ],
        out_specs=pl.BlockSpec((None, tq, hp * dh), lambda bi, g, qi: (bi, qi, g)),
        out_shape=jax.ShapeDtypeStruct((b, s, heads * dh), BF16),
        scratch_shapes=[pltpu.VMEM((hp, tq, dh), F32)],
        compiler_params=_params("parallel", "parallel", "arbitrary"),
        name="sb_attention",
    )(qkv, qkv, qkv)


def _hgrn_kernel(lbl_ref, gain_ref, q_ref, f_ref, i_ref, g_ref, o_ref, st_ref, *,
                 layer_j, heads, chunk, tt):
    @pl.when(pl.program_id(1) == 0)
    def _():
        st_ref[...] = jnp.zeros_like(st_ref)

    logits = lbl_ref[...]
    e = jnp.exp(logits - jnp.max(logits, axis=0, keepdims=True))
    sm = e / jnp.sum(e, axis=0, keepdims=True)
    lb = jnp.sum(sm[:layer_j + 1], axis=0, keepdims=True)
    gain = gain_ref[...]

    dk = q_ref.shape[1] // heads
    dv = i_ref.shape[1] // heads
    trow = lax.broadcasted_iota(jnp.int32, (chunk, chunk), 0)
    tcol = lax.broadcasted_iota(jnp.int32, (chunk, chunk), 1)
    causal = tcol <= trow
    lower = causal.astype(BF16)

    def chunk_body(c, _):
        r0 = pl.multiple_of(c * chunk, chunk)
        rows = pl.ds(r0, chunk)
        f = lb + (1.0 - lb) * jax.nn.sigmoid(f_ref[rows, :])
        log_f = jnp.log(f)
        k = 1.0 - f
        qf = _silu(q_ref[rows, :])
        g_cum = _split_dot_left(lower, log_f)
        g_last = g_cum[chunk - 1:chunk, :]
        q_dec = (qf * jnp.exp(g_cum)).astype(BF16)
        k_inv = (k * jnp.exp(-g_cum)).astype(BF16)
        k_end = (k * jnp.exp(g_last - g_cum)).astype(BF16)
        decay = jnp.exp(g_last)
        v_all = i_ref[rows, :]
        gate = _silu(g_ref[rows, :])
        hs = [{} for _ in range(heads)]

        def stage_a(h, ks, vs):
            st = st_ref[h]
            v_t = v_all[:, vs].T.astype(BF16)
            hs[h].update(st=st,
                         raw=_dot_nt(q_dec[:, ks], k_inv[:, ks]),
                         inter=_dot_nt(q_dec[:, ks], st.astype(BF16)),
                         upd=_dot(v_t, k_end[:, ks]))

        def stage_b(h, ks, vs):
            hs[h]["scores"] = jnp.where(causal, hs[h].pop("raw"), 0.0).astype(BF16)
            st_ref[h] = decay[:, ks] * hs[h].pop("st") + hs[h].pop("upd")

        def stage_c(h, ks, vs):
            hs[h]["o"] = _dot(hs[h].pop("scores"), v_all[:, vs].astype(BF16)) + hs[h].pop("inter")

        def stage_d(h, ks, vs):
            o = hs[h].pop("o")
            o = o * lax.rsqrt(jnp.mean(o * o, axis=-1, keepdims=True) + NORM_EPS) * gain[:, vs]
            o_ref[rows, vs] = (o * gate[:, vs]).astype(o_ref.dtype)

        stages = (stage_a, stage_b, stage_c, stage_d)
        for slot in range(heads + len(stages) - 1):
            for depth, stage in enumerate(stages):
                h = slot - depth
                if 0 <= h < heads:
                    stage(h, slice(h * dk, (h + 1) * dk), slice(h * dv, (h + 1) * dv))
        return 0

    lax.fori_loop(0, tt // chunk, chunk_body, 0)


def _hgrn2(hg, lb_logits, out_gain, layer_j, heads, tt):
    b, s, w4 = hg.shape
    kw = lb_logits.shape[1]
    vw = out_gain.shape[1]
    assert w4 == 2 * kw + 2 * vw and kw == vw
    nb = w4 // kw
    dk = kw // heads
    dv = vw // heads
    kern = functools.partial(_hgrn_kernel, layer_j=layer_j, heads=heads, chunk=HG_CHUNK, tt=tt)
    col = lambda k: pl.BlockSpec((None, tt, kw), lambda bi, ti: (bi, ti, k))
    assert nb == 4
    return pl.pallas_call(
        kern,
        grid=(b, s // tt),
        in_specs=[
            pl.BlockSpec(lb_logits.shape, lambda bi, ti: (0, 0)),
            pl.BlockSpec((1, vw), lambda bi, ti: (0, 0)),
            col(0), col(1), col(2), col(3),
        ],
        out_specs=pl.BlockSpec((None, tt, vw), lambda bi, ti: (bi, ti, 0)),
        out_shape=jax.ShapeDtypeStruct((b, s, vw), BF16),
        scratch_shapes=[pltpu.VMEM((heads, dv, dk), F32)],
        compiler_params=_params("parallel", "arbitrary"),
        name="hgrn2",
    )(lb_logits, out_gain, hg, hg, hg, hg)


def _residual_matmul_kernel(*refs, n_lhs):
    a_refs = refs[:n_lhs]
    w_refs = refs[n_lhs:2 * n_lhs]
    x_ref, g_ref, o_ref = refs[2 * n_lhs:]
    acc = _dot(a_refs[0][...], w_refs[0][...])
    for a_ref, w_ref in zip(a_refs[1:], w_refs[1:]):
        acc = acc + _dot(a_ref[...], w_ref[...])
    o_ref[...] = x_ref[...] + g_ref[...] * acc


def _residual_matmul(lhs_list, w, w_index, x2, mod4, layer, slot, seq, bm, bn, name):
    t, d = x2.shape
    n_lhs = len(lhs_list)
    tps = seq // bm
    in_specs = []
    row0 = 0
    w_specs = []
    for a in lhs_list:
        ka = a.shape[1]
        in_specs.append(pl.BlockSpec((bm, ka), lambda i, j: (i, 0)))
        assert row0 % ka == 0
        w_specs.append(pl.BlockSpec((None, ka, bn), functools.partial(
            lambda i, j, r: (w_index, r, j), r=row0 // ka)))
        row0 += ka
    assert row0 == w.shape[1]
    in_specs += w_specs
    in_specs += [
        pl.BlockSpec((bm, bn), lambda i, j: (i, j)),
        _mod_spec(layer, slot, d, bn, tps, True),
    ]
    return pl.pallas_call(
        functools.partial(_residual_matmul_kernel, n_lhs=n_lhs),
        grid=(t // bm, d // bn),
        in_specs=in_specs,
        out_specs=pl.BlockSpec((bm, bn), lambda i, j: (i, j)),
        out_shape=jax.ShapeDtypeStruct((t, d), F32),
        compiler_params=_params("parallel", "arbitrary"),
        name=name,
    )(*lhs_list, *([w] * n_lhs), x2, mod4)


def _ffn_up_kernel(x_ref, xh_ref, gain_ref, sc_ref, sh_ref, wg_ref, wv_ref, cwg_ref, cwv_ref,
                   cbg_ref, cbv_ref, o_ref, h_ref, hh_ref, eg_ref, ev_ref, *, bm, tiles_per_seq):
    i = pl.program_id(0)

    @pl.when(pl.program_id(1) == 0)
    def _():
        _norm_rows(x_ref, h_ref, gain_ref, sc_ref, sh_ref)
        _norm_rows(xh_ref, hh_ref, gain_ref, sc_ref, sh_ref)

    first = (i % tiles_per_seq) == 0
    h = h_ref[...]
    hh = hh_ref[...]
    for w_ref, e_ref in ((wg_ref, eg_ref), (wv_ref, ev_ref)):
        w = w_ref[...]
        e_ref[pl.ds(0, HALO_ROWS), :] = jnp.where(first, 0.0, _dot(hh, w))
        e_ref[pl.ds(HALO_ROWS, bm), :] = _dot(h, w)

    def conv(e_ref, cw_ref, cb_ref):
        out = cb_ref[...]
        for tap in range(CONV_WIDTH):
            off = HALO_ROWS - (CONV_WIDTH - 1) + tap
            out = out + cw_ref[tap:tap + 1, :] * e_ref[pl.ds(off, bm), :]
        return out

    gate = conv(eg_ref, cwg_ref, cbg_ref)
    val = conv(ev_ref, cwv_ref, cbv_ref)
    o_ref[...] = (_silu(gate) * val).astype(o_ref.dtype)


def _ffn_up(x2, gain, mod4, layer, w_up, conv_w, conv_b, seq, bm, bn):
    t, d = x2.shape
    f = w_up.shape[2] // 2
    nj = f // bn
    tps = seq // bm
    hb = bm // HALO_ROWS
    depth = conv_w.shape[0]
    conv_b3 = conv_b.reshape(depth, 1, 2 * f)
    kern = functools.partial(_ffn_up_kernel, bm=bm, tiles_per_seq=tps)
    e_buf = pltpu.VMEM((bm + HALO_ROWS, bn), F32)
    return pl.pallas_call(
        kern,
        grid=(t // bm, nj),
        in_specs=[
            pl.BlockSpec((bm, d), lambda i, j: (i, 0)),
            pl.BlockSpec((HALO_ROWS, d), lambda i, j: (jnp.maximum(i * hb - 1, 0), 0)),
            pl.BlockSpec((1, d), lambda i, j: (0, 0)),
            _mod_spec(layer, 4, d, d, tps, False),
            _mod_spec(layer, 3, d, d, tps, False),
            pl.BlockSpec((None, d, bn), lambda i, j: (layer, 0, j)),
            pl.BlockSpec((None, d, bn), lambda i, j: (layer, 0, nj + j)),
            pl.BlockSpec((None, CONV_WIDTH, bn), lambda i, j: (layer, 0, j)),
            pl.BlockSpec((None, CONV_WIDTH, bn), lambda i, j: (layer, 0, nj + j)),
            pl.BlockSpec((None, 1, bn), lambda i, j: (layer, 0, j)),
            pl.BlockSpec((None, 1, bn), lambda i, j: (layer, 0, nj + j)),
        ],
        out_specs=pl.BlockSpec((bm, bn), lambda i, j: (i, j)),
        out_shape=jax.ShapeDtypeStruct((t, f), BF16),
        scratch_shapes=[
            pltpu.VMEM((bm, d), BF16),
            pltpu.VMEM((HALO_ROWS, d), BF16),
            e_buf, e_buf,
        ],
        compiler_params=_params("parallel", "arbitrary"),
        name="ffn_up",
    )(x2, x2, gain, mod4, mod4, w_up, w_up, conv_w, conv_w, conv_b3, conv_b3)


def _gelu(x):
    return 0.5 * x * (1.0 + lax.erf(x * (2.0 ** -0.5)))


def _sg_in_kernel(x_ref, gain_ref, sc_ref, sh_ref, w_ref, vg_ref, vb_ref, u_ref, v_ref, h_ref):
    j = pl.program_id(1)

    @pl.when(j == 0)
    def _():
        _norm_rows(x_ref, h_ref, gain_ref, sc_ref, sh_ref)

    z = _gelu(_dot(h_ref[...], w_ref[...]))

    @pl.when(j == 0)
    def _():
        u_ref[...] = z.astype(u_ref.dtype)

    @pl.when(j == 1)
    def _():
        zc = z - jnp.mean(z, axis=-1, keepdims=True)
        y = zc * lax.rsqrt(jnp.mean(zc * zc, axis=-1, keepdims=True) + NORM_EPS)
        v_ref[...] = (y * vg_ref[...] + vb_ref[...]).astype(v_ref.dtype)


def _sg_in(x2, gain, mod4, layer, w_in, v_gain, v_bias, seq, bm):
    t, d = x2.shape
    width = w_in.shape[1] // 2
    tps = seq // bm
    out = jax.ShapeDtypeStruct((t, width), BF16)
    return pl.pallas_call(
        _sg_in_kernel,
        grid=(t // bm, 2),
        in_specs=[
            pl.BlockSpec((bm, d), lambda i, j: (i, 0)),
            pl.BlockSpec((1, d), lambda i, j: (0, 0)),
            _mod_spec(layer, 1, d, d, tps, False),
            _mod_spec(layer, 0, d, d, tps, False),
            pl.BlockSpec((d, width), lambda i, j: (0, j)),
            pl.BlockSpec((1, width), lambda i, j: (0, 0)),
            pl.BlockSpec((1, width), lambda i, j: (0, 0)),
        ],
        out_specs=[pl.BlockSpec((bm, width), lambda i, j: (i, 0)),
                   pl.BlockSpec((bm, width), lambda i, j: (i, 0))],
        out_shape=[out, out],
        scratch_shapes=[pltpu.VMEM((bm, d), BF16)],
        compiler_params=_params("parallel", "arbitrary"),
        name="sg_in",
    )(x2, gain, mod4, mod4, w_in, v_gain, v_bias)


def _sg_out_kernel(u_ref, v_ref, wp_ref, bp_ref, w_ref, x_ref, g_ref, o_ref, m_ref, *,
                   bm, groups, chunk):
    @pl.when(pl.program_id(1) == 0)
    def _():
        gw = u_ref.shape[1] // groups
        row = lax.broadcasted_iota(jnp.int32, (chunk, chunk), 0)
        col = lax.broadcasted_iota(jnp.int32, (chunk, chunk), 1)
        causal = col <= row
        for g in range(groups):
            w_pos = jnp.where(causal, wp_ref[g], 0.0).astype(BF16)
            bias = bp_ref[:, g:g + 1]
            cs = slice(g * gw, (g + 1) * gw)
            for r in range(bm // chunk):
                rs = slice(r * chunk, (r + 1) * chunk)
                mixed = _dot(w_pos, v_ref[rs, cs]) + bias
                m_ref[rs, cs] = (u_ref[rs, cs].astype(F32) * mixed).astype(BF16)

    o_ref[...] = x_ref[...] + g_ref[...] * _dot(m_ref[...], w_ref[...])


def _sg_out(u, v, w_pos, b_pos_t, w_out, x2, mod4, layer, seq, bm, bn):
    t, d = x2.shape
    width = u.shape[1]
    tps = seq // bm
    kern = functools.partial(_sg_out_kernel, bm=bm, groups=SG_GROUPS, chunk=SG_CHUNK)
    return pl.pallas_call(
        kern,
        grid=(t // bm, d // bn),
        in_specs=[
            pl.BlockSpec((bm, width), lambda i, j: (i, 0)),
            pl.BlockSpec((bm, width), lambda i, j: (i, 0)),
            pl.BlockSpec(w_pos.shape, lambda i, j: (0, 0, 0)),
            pl.BlockSpec(b_pos_t.shape, lambda i, j: (0, 0)),
            pl.BlockSpec((width, bn), lambda i, j: (0, j)),
            pl.BlockSpec((bm, bn), lambda i, j: (i, j)),
            _mod_spec(layer, 2, d, bn, tps, True),
        ],
        out_specs=pl.BlockSpec((bm, bn), lambda i, j: (i, j)),
        out_shape=jax.ShapeDtypeStruct((t, d), F32),
        scratch_shapes=[pltpu.VMEM((bm, width), BF16)],
        compiler_params=_params("parallel", "arbitrary"),
        name="sg_out",
    )(u, v, w_pos, b_pos_t, w_out, x2, mod4)


def _final_norm_kernel(x_ref, gain_ref, o_ref):
    x = x_ref[...]
    o_ref[...] = x * lax.rsqrt(jnp.mean(x * x, axis=-1, keepdims=True) + NORM_EPS) * gain_ref[...]


def _final_norm(x2, gain, bm):
    t, d = x2.shape
    return pl.pallas_call(
        _final_norm_kernel,
        grid=(t // bm,),
        in_specs=[pl.BlockSpec((bm, d), lambda i: (i, 0)),
                  pl.BlockSpec((1, d), lambda i: (0, 0))],
        out_specs=pl.BlockSpec((bm, d), lambda i: (i, 0)),
        out_shape=jax.ShapeDtypeStruct((t, d), F32),
        compiler_params=_params("parallel"),
        name="final_norm",
    )(x2, gain)


def kernel(x, c, ada_w, ada_b, mix_norm, ffn_norm, par_w_in, par_w_out, hg_lb_logits, hg_out_norm, sg_w_in, sg_v_gain, sg_v_bias, sg_w_pos, sg_b_pos, sg_w_out, ffn_up, ffn_conv_w, ffn_conv_b, ffn_down, final_norm):
    b, s, d = x.shape
    depth = ada_w.shape[0]
    t = b * s
    sb_w = d // 2
    bm = min(ROW_TILE, s)
    bm_sg = min(SG_ROW_TILE, s)

    c_pad = jnp.pad(c, ((0, ADA_PAD_ROWS - b), (0, 0)))
    mod = _ada_mod(c_pad, ada_w, ada_b)
    mod4 = mod.reshape(depth, ADA_PAD_ROWS, 1, 6 * d)

    ffn_up_bf16 = ffn_up.astype(BF16)
    ffn_down_bf16 = ffn_down.astype(BF16)
    x2 = x.reshape(t, d)
    for layer in range(depth):
        j = layer // 2
        gain1 = mix_norm[layer].reshape(1, d)
        if layer % 2 == 0:
            qkv, hg = _par_proj(x2, gain1, mod4, layer, par_w_in[j].astype(BF16), 3 * sb_w,
                                s, bm, 1024)
            o_sb = _sb_attention(qkv.reshape(b, s, 3 * sb_w), SB_HEADS, min(SB_QUERY_TILE, s),
                                 SB_HEADS_PER_STEP)
            o_hg = _hgrn2(hg.reshape(b, s, hg.shape[1]), hg_lb_logits,
                          hg_out_norm[j].reshape(1, -1), j, HG_HEADS, min(HG_TIME_TILE, s))
            x2 = _residual_matmul([o_sb.reshape(t, sb_w), o_hg.reshape(t, -1)],
                                  par_w_out.astype(BF16), j, x2, mod4, layer, 2, s, bm, 1024,
                                  "par_out")
        else:
            u, v = _sg_in(x2, gain1, mod4, layer, sg_w_in[j].astype(BF16),
                          sg_v_gain[j].reshape(1, -1), sg_v_bias[j].reshape(1, -1), s, bm_sg)
            x2 = _sg_out(u, v, sg_w_pos[j], sg_b_pos[j].T, sg_w_out[j].astype(BF16), x2, mod4,
                         layer, s, bm_sg, 1024)
        act = _ffn_up(x2, ffn_norm[layer].reshape(1, d), mod4, layer, ffn_up_bf16,
                      ffn_conv_w, ffn_conv_b, s, bm, 512)
        x2 = _residual_matmul([act], ffn_down_bf16, layer, x2, mod4, layer, 5, s, bm, 512,
                              "ffn_down")
    out = _final_norm(x2, final_norm.reshape(1, d), bm_sg)
    return out.reshape(b, s, d)
```
